```python
import math
import jax, jax.numpy as jnp
from jax import lax
import numpy as np

D_MODEL = 1024
BATCH = 8
SEQ = 4096
DEPTH = 2

CHUNK = 64
PLE_DIM = 256
LRU_WIDTH = 1280
LRU_HEADS = 10
LRU_HEAD_DIM = LRU_WIDTH // LRU_HEADS
LRU_CONV = 4
LRU_C = 8.0
S5_WIDTH = D_MODEL
S5_GROUP = 16
S5_GROUPS = S5_WIDTH // S5_GROUP
S5_STATE = 64
D_FF = 3 * D_MODEL
FFN_CONV = 3
IN_COLS = 2 * LRU_WIDTH + S5_WIDTH + 2 * D_MODEL
EPS = 1e-6

kernel_name = "griffin_s5_convffn_hybrid"


def rmsnorm(x, g):
    xf = x.astype(jnp.float32)
    y = xf * lax.rsqrt(jnp.mean(xf * xf, axis=-1, keepdims=True) + EPS)
    return (y * g.astype(jnp.float32)).astype(x.dtype)


def causal_dwconv(x, w, b):
    k = w.shape[0]
    s = x.shape[1]
    xp = jnp.pad(x, ((0, 0), (k - 1, 0), (0, 0)))
    y = b
    for j in range(k):
        y = y + xp[:, j:j + s] * w[j]
    return y


def rglru_branch(xa, gate_br, conv_w, conv_b, gx_w, gx_b, ga_w, ga_b, lam):
    xc = causal_dwconv(xa, conv_w, conv_b)
    bsz, s, _ = xc.shape
    xh = xc.reshape(bsz, s, LRU_HEADS, LRU_HEAD_DIM)
    gx = jax.nn.sigmoid(jnp.einsum('bshi,hij->bshj', xh, gx_w).reshape(bsz, s, LRU_WIDTH) + gx_b)
    ga = jax.nn.sigmoid(jnp.einsum('bshi,hij->bshj', xh, ga_w).reshape(bsz, s, LRU_WIDTH) + ga_b)
    log_a = (LRU_C * ga.astype(jnp.float32)) * jax.nn.log_sigmoid(lam.astype(jnp.float32))
    a = jnp.exp(log_a)
    mult = jnp.sqrt(jnp.maximum(-jnp.expm1(2.0 * log_a), 1e-12))
    u = mult * (gx * xc).astype(jnp.float32)

    def combine(e1, e2):
        return (e1[0] * e2[0], e2[0] * e1[1] + e2[1])

    _, h = lax.associative_scan(combine, (a, u), axis=1)
    return h.astype(xa.dtype) * jax.nn.gelu(gate_br)


def s5_ssm(u, a_re, a_im, log_dt, b_re, b_im, c_re, c_im, d):
    f32 = jnp.float32
    a_re = a_re.astype(f32); a_im = a_im.astype(f32)
    b_re = b_re.astype(f32); b_im = b_im.astype(f32)
    c_re = c_re.astype(f32); c_im = c_im.astype(f32)
    dt = jnp.exp(log_dt.astype(f32))[:, None]
    mag = jnp.exp(a_re * dt)
    lb_re = mag * jnp.cos(a_im * dt)
    lb_im = mag * jnp.sin(a_im * dt)
    nr = lb_re - 1.0
    ni = lb_im
    den = a_re * a_re + a_im * a_im
    coef_re = (nr * a_re + ni * a_im) / den
    coef_im = (ni * a_re - nr * a_im) / den
    bb_re = coef_re[..., None] * b_re - coef_im[..., None] * b_im
    bb_im = coef_re[..., None] * b_im + coef_im[..., None] * b_re

    bsz, s, _ = u.shape
    nc = s // CHUNK
    uc = u.astype(f32).reshape(bsz, nc, CHUNK, S5_GROUPS, S5_GROUP).transpose(1, 2, 0, 3, 4)
    a_el_re = jnp.broadcast_to(lb_re, (CHUNK, 1, S5_GROUPS, S5_STATE))
    a_el_im = jnp.broadcast_to(lb_im, (CHUNK, 1, S5_GROUPS, S5_STATE))

    def combine(e1, e2):
        a1r, a1i, b1r, b1i = e1
        a2r, a2i, b2r, b2i = e2
        return (a2r * a1r - a2i * a1i,
                a2r * a1i + a2i * a1r,
                a2r * b1r - a2i * b1i + b2r,
                a2r * b1i + a2i * b1r + b2i)

    def step(carry, u_blk):
        c_r, c_i = carry
        bu_re = jnp.einsum('lbgc,gpc->lbgp', u_blk, bb_re)
        bu_im = jnp.einsum('lbgc,gpc->lbgp', u_blk, bb_im)
        ar, ai, sr, si = lax.associative_scan(combine, (a_el_re, a_el_im, bu_re, bu_im), axis=0)
        x_re = sr + ar * c_r[None] - ai * c_i[None]
        x_im = si + ar * c_i[None] + ai * c_r[None]
        y = jnp.einsum('lbgp,gcp->lbgc', x_re, c_re) - jnp.einsum('lbgp,gcp->lbgc', x_im, c_im)
        return (x_re[-1], x_im[-1]), y

    init = (jnp.zeros((bsz, S5_GROUPS, S5_STATE), f32), jnp.zeros((bsz, S5_GROUPS, S5_STATE), f32))
    _, y = lax.scan(step, init, uc)
    y = y.transpose(2, 0, 1, 3, 4).reshape(bsz, s, S5_WIDTH)
    return (y + d.astype(f32) * u.astype(f32)).astype(u.dtype)


def setup_inputs(seed: int = 0) -> dict:
    key = jax.random.key(seed)
    ks = jax.random.split(key, 40)
    nrm = lambda k, shape, scale: jax.random.normal(k, shape, jnp.float32) * scale
    x = jax.random.normal(ks[0], (BATCH, SEQ, D_MODEL), jnp.float32)
    p = jax.random.normal(ks[1], (DEPTH, BATCH, SEQ, PLE_DIM), jnp.float32)
    g_mix = 1.0 + nrm(ks[2], (DEPTH, D_MODEL), 0.02)
    w_in = nrm(ks[3], (DEPTH, D_MODEL, IN_COLS), D_MODEL ** -0.5)
    conv_a_w = nrm(ks[4], (DEPTH, LRU_CONV, LRU_WIDTH), LRU_CONV ** -0.5)
    conv_a_b = nrm(ks[5], (DEPTH, LRU_WIDTH), 0.01)
    gate_x_w = nrm(ks[6], (DEPTH, LRU_HEADS, LRU_HEAD_DIM, LRU_HEAD_DIM), LRU_HEAD_DIM ** -0.5)
    gate_x_b = nrm(ks[7], (DEPTH, LRU_WIDTH), 0.01)
    gate_a_w = nrm(ks[8], (DEPTH, LRU_HEADS, LRU_HEAD_DIM, LRU_HEAD_DIM), LRU_HEAD_DIM ** -0.5)
    gate_a_b = nrm(ks[9], (DEPTH, LRU_WIDTH), 0.01)
    a_target = jax.random.uniform(ks[10], (DEPTH, LRU_WIDTH), jnp.float32, 0.9, 0.999)
    a0 = a_target ** (1.0 / LRU_C)
    lru_lambda = jnp.log(a0) - jnp.log1p(-a0)
    w_a_out = nrm(ks[11], (DEPTH, LRU_WIDTH, D_MODEL), LRU_WIDTH ** -0.5)
    s5_a_re = -0.5 + nrm(ks[12], (DEPTH, S5_GROUPS, S5_STATE), 0.01)
    s5_a_im = jnp.pi * jnp.arange(S5_STATE, dtype=jnp.float32) + nrm(ks[13], (DEPTH, S5_GROUPS, S5_STATE), 0.01)
    s5_log_dt = jax.random.uniform(ks[14], (DEPTH, S5_GROUPS), jnp.float32, math.log(1e-3), math.log(1e-1))
    s5_b_re = nrm(ks[15], (DEPTH, S5_GROUPS, S5_STATE, S5_GROUP), (2 * S5_GROUP) ** -0.5)
    s5_b_im = nrm(ks[16], (DEPTH, S5_GROUPS, S5_STATE, S5_GROUP), (2 * S5_GROUP) ** -0.5)
    s5_c_re = nrm(ks[17], (DEPTH, S5_GROUPS, S5_GROUP, S5_STATE), (2 * S5_STATE) ** -0.5)
    s5_c_im = nrm(ks[18], (DEPTH, S5_GROUPS, S5_GROUP, S5_STATE), (2 * S5_STATE) ** -0.5)
    s5_d = nrm(ks[19], (DEPTH, S5_WIDTH), 1.0)
    w_glu = nrm(ks[20], (DEPTH, S5_WIDTH, 2 * D_MODEL), S5_WIDTH ** -0.5)
    b_glu = nrm(ks[21], (DEPTH, 2 * D_MODEL), 0.01)
    w_o = nrm(ks[22], (DEPTH, D_MODEL, D_MODEL), D_MODEL ** -0.5)
    g_ffn = 1.0 + nrm(ks[23], (DEPTH, D_MODEL), 0.02)
    w_up = nrm(ks[24], (DEPTH, D_MODEL, 2 * D_FF), D_MODEL ** -0.5)
    conv_f_w = nrm(ks[25], (DEPTH, FFN_CONV, 2 * D_FF), FFN_CONV ** -0.5)
    conv_f_b = nrm(ks[26], (DEPTH, 2 * D_FF), 0.01)
    w_down = nrm(ks[27], (DEPTH, D_FF, D_MODEL), D_FF ** -0.5)
    g_ple = 1.0 + nrm(ks[28], (DEPTH, D_MODEL), 0.02)
    w_ple_gate = nrm(ks[29], (DEPTH, D_MODEL, D_MODEL), D_MODEL ** -0.5)
    w_ple_proj = nrm(ks[30], (DEPTH, PLE_DIM, D_MODEL), PLE_DIM ** -0.5)
    g_final = 1.0 + nrm(ks[31], (D_MODEL,), 0.02)
    return {"x": x, "p": p, "g_mix": g_mix, "w_in": w_in, "conv_a_w": conv_a_w, "conv_a_b": conv_a_b,
            "gate_x_w": gate_x_w, "gate_x_b": gate_x_b, "gate_a_w": gate_a_w, "gate_a_b": gate_a_b,
            "lru_lambda": lru_lambda, "w_a_out": w_a_out, "s5_a_re": s5_a_re, "s5_a_im": s5_a_im,
            "s5_log_dt": s5_log_dt, "s5_b_re": s5_b_re, "s5_b_im": s5_b_im, "s5_c_re": s5_c_re,
            "s5_c_im": s5_c_im, "s5_d": s5_d, "w_glu": w_glu, "b_glu": b_glu, "w_o": w_o,
            "g_ffn": g_ffn, "w_up": w_up, "conv_f_w": conv_f_w, "conv_f_b": conv_f_b, "w_down": w_down,
            "g_ple": g_ple, "w_ple_gate": w_ple_gate, "w_ple_proj": w_ple_proj, "g_final": g_final}


def reference(x, p, g_mix, w_in, conv_a_w, conv_a_b, gate_x_w, gate_x_b, gate_a_w, gate_a_b,
              lru_lambda, w_a_out, s5_a_re, s5_a_im, s5_log_dt, s5_b_re, s5_b_im, s5_c_re,
              s5_c_im, s5_d, w_glu, b_glu, w_o, g_ffn, w_up, conv_f_w, conv_f_b, w_down,
              g_ple, w_ple_gate, w_ple_proj, g_final):
    o1 = LRU_WIDTH
    o2 = 2 * LRU_WIDTH
    o3 = o2 + S5_WIDTH
    o4 = o3 + D_MODEL
    for i in range(DEPTH):
        h = rmsnorm(x, g_mix[i])
        z = h @ w_in[i]
        xa, gate_br, ub = z[..., :o1], z[..., o1:o2], z[..., o2:o3]
        m_a, m_b = z[..., o3:o4], z[..., o4:]
        ya = rglru_branch(xa, gate_br, conv_a_w[i], conv_a_b[i], gate_x_w[i], gate_x_b[i],
                          gate_a_w[i], gate_a_b[i], lru_lambda[i]) @ w_a_out[i]
        ys = jax.nn.gelu(s5_ssm(ub, s5_a_re[i], s5_a_im[i], s5_log_dt[i], s5_b_re[i], s5_b_im[i],
                                s5_c_re[i], s5_c_im[i], s5_d[i]))
        glu = ys @ w_glu[i] + b_glu[i]
        yb = glu[..., :D_MODEL] * jax.nn.sigmoid(glu[..., D_MODEL:])
        merged = jax.nn.sigmoid(m_a) * ya + jax.nn.sigmoid(m_b) * yb
        x = x + merged @ w_o[i]
        h = rmsnorm(x, g_ffn[i])
        up = causal_dwconv(h @ w_up[i], conv_f_w[i], conv_f_b[i])
        x = x + (jax.nn.gelu(up[..., :D_FF]) * up[..., D_FF:]) @ w_down[i]
        gate = jax.nn.sigmoid(rmsnorm(x, g_ple[i]) @ w_ple_gate[i])
        x = x + gate * (p[i] @ w_ple_proj[i])
    return rmsnorm(x, g_final)
```

```python
import functools
import math

import jax
import jax.numpy as jnp
from jax import lax
from jax.experimental import pallas as pl
from jax.experimental.pallas import tpu as pltpu

D_MODEL = 1024
BATCH = 8
SEQ = 4096
DEPTH = 2
PLE_DIM = 256
LRU_WIDTH = 1280
LRU_HEADS = 10
LRU_HEAD_DIM = 128
LRU_CONV = 4
LRU_C = 8.0
S5_GROUP = 16
S5_GROUPS = 64
S5_STATE = 64
D_FF = 3 * D_MODEL
FFN_CONV = 3
EPS = 1e-6

ROWS = SEQ * BATCH
PAIR = 2 * LRU_HEAD_DIM
N_PAIRS = LRU_WIDTH // PAIR
S5_CHUNK = 16
S5_VEC = S5_CHUNK * S5_GROUP
S5_NCHUNK = SEQ // S5_CHUNK
S5_ROWS = S5_NCHUNK * BATCH
S5_REAL = 2 * S5_STATE

MIXER_ROWS = 256
FFN_ROWS = 256
FFN_COLS = 512
S5_GROUPS_PER_STEP = 2
VMEM_LIMIT_BYTES = 56 * 1024 * 1024

_GELU_C = math.sqrt(2.0 / math.pi)


def _gelu(x):
    return 0.5 * x * (1.0 + jnp.tanh(_GELU_C * (x + 0.044715 * (x * x * x))))


def _sigmoid(x):
    return 0.5 * jnp.tanh(0.5 * x) + 0.5


def _rmsnorm(x, g):
    ms = jnp.mean(x * x, axis=-1, keepdims=True)
    return x * lax.rsqrt(ms + EPS) * g


def _dot(a, b):
    return jnp.dot(a, b, preferred_element_type=jnp.float32)


def _const_spec(shape):
    zeros = (0,) * len(shape)
    return pl.BlockSpec(shape, lambda i: zeros, pipeline_mode=pl.Buffered(1))


def _row_spec(rows, cols):
    return pl.BlockSpec((rows, cols), lambda i: (i, 0))


def _mixer_kernel(x_ref, g_ref, wxa_ref, wgb_ref, wub_ref, wma_ref, wmb_ref,
                  cw_ref, cb_ref, gw_ref, gxb_ref, gab_ref, lam_ref, wao_ref,
                  ub_ref, ap_ref, sb_ref,
                  ext_s, a_s, u_s, hg_s, hc_s):
    rows = x_ref.shape[0]
    halo = (LRU_CONV - 1) * BATCH

    @pl.when(pl.program_id(0) == 0)
    def _():
        ext_s[:, 0:halo, :] = jnp.zeros((N_PAIRS, halo, PAIR), jnp.float32)
        hc_s[...] = jnp.zeros_like(hc_s)

    hb = _rmsnorm(x_ref[...], g_ref[...]).astype(jnp.bfloat16)

    for j in range(N_PAIRS):
        xa = _dot(hb, wxa_ref[j])
        ext_s[j, halo:halo + rows, :] = xa
        xc = cb_ref[j] + cw_ref[j, 3:4, :] * xa
        for k in range(LRU_CONV - 1):
            xc = xc + cw_ref[j, k:k + 1, :] * ext_s[j, k * BATCH:k * BATCH + rows, :]
        ext_s[j, 0:halo, :] = ext_s[j, rows:rows + halo, :]
        xcb = xc.astype(jnp.bfloat16)
        lam = lam_ref[j]
        log_sig = jnp.minimum(lam, 0.0) - jnp.log1p(jnp.exp(-jnp.abs(lam)))
        for hh in range(2):
            lo, hi = hh * LRU_HEAD_DIM, (hh + 1) * LRU_HEAD_DIM
            gates = _dot(xcb[:, lo:hi], gw_ref[2 * j + hh])
            gx = _sigmoid(gates[:, :LRU_HEAD_DIM] + gxb_ref[j, :, lo:hi])
            ga = _sigmoid(gates[:, LRU_HEAD_DIM:] + gab_ref[j, :, lo:hi])
            log_a = (LRU_C * ga) * log_sig[:, lo:hi]
            a = jnp.exp(log_a)
            mult = jnp.sqrt(jnp.maximum((1.0 - a) * (1.0 + a), 1e-12))
            a_s[:, j * PAIR + lo:j * PAIR + hi] = a
            u_s[:, j * PAIR + lo:j * PAIR + hi] = mult * (gx * xc[:, lo:hi])

    def step(t, h):
        r = pl.multiple_of(t * BATCH, BATCH)
        h = a_s[pl.ds(r, BATCH), :] * h + u_s[pl.ds(r, BATCH), :]
        u_s[pl.ds(r, BATCH), :] = h
        return h

    hc_s[...] = lax.fori_loop(0, rows // BATCH, step, hc_s[...], unroll=8)

    for j in range(N_PAIRS):
        gate_br = _dot(hb, wgb_ref[j])
        hg_s[:, j * PAIR:(j + 1) * PAIR] = (
            u_s[:, j * PAIR:(j + 1) * PAIR] * _gelu(gate_br)).astype(jnp.bfloat16)

    ya = _dot(hg_s[...], wao_ref[...])
    ap_ref[...] = (_sigmoid(_dot(hb, wma_ref[...])) * ya).astype(ap_ref.dtype)
    sb_ref[...] = _sigmoid(_dot(hb, wmb_ref[...])).astype(sb_ref.dtype)
    ub_ref[...] = _dot(hb, wub_ref[...]).astype(ub_ref.dtype)


def _mixer_call(xt, prm):
    rows = MIXER_ROWS
    halo = (LRU_CONV - 1) * BATCH
    out = jax.ShapeDtypeStruct((ROWS, D_MODEL), jnp.bfloat16)
    weights = [prm[k] for k in ("g_mix", "wxa", "wgb", "wub", "wma", "wmb", "cw", "cb",
                                "gw", "gxb", "gab", "lam", "wao")]
    return pl.pallas_call(
        _mixer_kernel,
        grid=(ROWS // rows,),
        in_specs=[_row_spec(rows, D_MODEL)] + [_const_spec(w.shape) for w in weights],
        out_specs=[_row_spec(rows, D_MODEL)] * 3,
        out_shape=[out, out, out],
        scratch_shapes=[
            pltpu.VMEM((N_PAIRS, halo + rows, PAIR), jnp.float32),
            pltpu.VMEM((rows, LRU_WIDTH), jnp.float32),
            pltpu.VMEM((rows, LRU_WIDTH), jnp.float32),
            pltpu.VMEM((rows, LRU_WIDTH), jnp.bfloat16),
            pltpu.VMEM((BATCH, LRU_WIDTH), jnp.float32),
        ],
        compiler_params=pltpu.CompilerParams(
            dimension_semantics=("arbitrary",), vmem_limit_bytes=VMEM_LIMIT_BYTES),
        name="mixer",
    )(xt, *weights)


def _s5_kernel(u_ref, wu_ref, tw_ref, coef_ref, y_ref, v_s, vsw_s, xs_s):
    ngroups = u_ref.shape[0]
    for g in range(ngroups):
        v = _dot(u_ref[g], wu_ref[g])
        v_s[g] = v
        vsw_s[g] = pltpu.roll(v, S5_STATE, axis=1)

    def step(k, carry):
        r = pl.multiple_of(k * BATCH, BATCH)
        new = []
        for g in range(ngroups):
            x, xsw = carry[g]
            xs_s[g, pl.ds(r, BATCH), :] = x
            ar = coef_ref[g, 0]
            ai = coef_ref[g, 1]
            xn = ar * x + ai * xsw + v_s[g, pl.ds(r, BATCH), :]
            xswn = ar * xsw - ai * x + vsw_s[g, pl.ds(r, BATCH), :]
            new.append((xn, xswn))
        return tuple(new)

    zero = jnp.zeros((BATCH, S5_REAL), jnp.float32)
    lax.fori_loop(0, S5_NCHUNK, step, tuple((zero, zero) for _ in range(ngroups)), unroll=8)

    for g in range(ngroups):
        y = _dot(u_ref[g], tw_ref[g, 0:S5_VEC, :])
        y = y + _dot(xs_s[g].astype(jnp.bfloat16), tw_ref[g, S5_VEC:, :])
        y_ref[g] = y.astype(y_ref.dtype)


def _s5_call(u_grp, prm):
    gp = S5_GROUPS_PER_STEP
    blk = lambda *shape: pl.BlockSpec((gp,) + shape, lambda i: (i,) + (0,) * len(shape))
    return pl.pallas_call(
        _s5_kernel,
        grid=(S5_GROUPS // gp,),
        in_specs=[blk(S5_ROWS, S5_VEC), blk(S5_VEC, S5_REAL), blk(S5_VEC + S5_REAL, S5_VEC),
                  blk(2, BATCH, S5_REAL)],
        out_specs=blk(S5_ROWS, S5_VEC),
        out_shape=jax.ShapeDtypeStruct((S5_GROUPS, S5_ROWS, S5_VEC), jnp.bfloat16),
        scratch_shapes=[pltpu.VMEM((gp, S5_ROWS, S5_REAL), jnp.float32)] * 3,
        compiler_params=pltpu.CompilerParams(
            dimension_semantics=("arbitrary",), vmem_limit_bytes=VMEM_LIMIT_BYTES),
        name="s5",
    )(u_grp, prm["s5_wu"], prm["s5_tw"], prm["s5_coef"])


def _ffn_kernel(x_ref, y_ref, ap_ref, sb_ref, p_ref,
                wglu_ref, bglu_ref, wo_ref, gffn_ref, wup_ref, cfw_ref, cfb_ref, wdn_ref,
                gple_ref, wpg_ref, wpp_ref, gfin_ref,
                o_ref, ext_s, act_s, fc_s, *, final):
    rows = x_ref.shape[0]
    halo = (FFN_CONV - 1) * BATCH

    @pl.when(pl.program_id(0) == 0)
    def _():
        fc_s[...] = jnp.zeros_like(fc_s)

    ys = _gelu(y_ref[...].astype(jnp.float32)).astype(jnp.bfloat16)
    glu_a = _dot(ys, wglu_ref[:, 0:D_MODEL]) + bglu_ref[:, 0:D_MODEL]
    glu_b = _dot(ys, wglu_ref[:, D_MODEL:]) + bglu_ref[:, D_MODEL:]
    merged = ap_ref[...].astype(jnp.float32) + sb_ref[...].astype(jnp.float32) * (glu_a * _sigmoid(glu_b))
    x1 = x_ref[...] + _dot(merged.astype(jnp.bfloat16), wo_ref[...])

    h2 = _rmsnorm(x1, gffn_ref[...]).astype(jnp.bfloat16)

    def conv_block(slot, col):
        up = _dot(h2, wup_ref[:, col:col + FFN_COLS])
        ext_s[slot, 0:halo, :] = fc_s[:, col:col + FFN_COLS]
        ext_s[slot, halo:halo + rows, :] = up
        fc_s[:, col:col + FFN_COLS] = up[rows - halo:rows, :]
        out = cfb_ref[:, col:col + FFN_COLS] + cfw_ref[2:3, col:col + FFN_COLS] * up
        for k in range(FFN_CONV - 1):
            out = out + cfw_ref[k:k + 1, col:col + FFN_COLS] * ext_s[slot, k * BATCH:k * BATCH + rows, :]
        return out

    for c in range(D_FF // FFN_COLS):
        col = c * FFN_COLS
        act = _gelu(conv_block(0, col)) * conv_block(1, D_FF + col)
        act_s[:, col:col + FFN_COLS] = act.astype(jnp.bfloat16)

    x2 = x1 + _dot(act_s[...], wdn_ref[...])

    h3 = _rmsnorm(x2, gple_ref[...]).astype(jnp.bfloat16)
    gate = _sigmoid(_dot(h3, wpg_ref[...]))
    x3 = x2 + gate * _dot(p_ref[...], wpp_ref[...])
    if final:
        x3 = _rmsnorm(x3, gfin_ref[...])
    o_ref[...] = x3


def _ffn_call(xt, y, ap, sb, pt, prm, final):
    rows = FFN_ROWS
    halo = (FFN_CONV - 1) * BATCH
    weights = [prm[k] for k in ("wglu", "bglu", "wo", "g_ffn", "wup", "cfw", "cfb", "wdn",
                                "g_ple", "wpg", "wpp", "g_final")]
    return pl.pallas_call(
        functools.partial(_ffn_kernel, final=final),
        grid=(ROWS // rows,),
        in_specs=[_row_spec(rows, D_MODEL)] * 4 + [_row_spec(rows, PLE_DIM)]
        + [_const_spec(w.shape) for w in weights],
        out_specs=_row_spec(rows, D_MODEL),
        out_shape=jax.ShapeDtypeStruct((ROWS, D_MODEL), jnp.float32),
        scratch_shapes=[
            pltpu.VMEM((2, halo + rows, FFN_COLS), jnp.float32),
            pltpu.VMEM((rows, D_FF), jnp.bfloat16),
            pltpu.VMEM((halo, 2 * D_FF), jnp.float32),
        ],
        compiler_params=pltpu.CompilerParams(
            dimension_semantics=("arbitrary",), vmem_limit_bytes=VMEM_LIMIT_BYTES),
        name="ffn_final" if final else "ffn",
    )(xt, y, ap, sb, pt, *weights)


def _s5_matrices(a_re, a_im, log_dt, b_re, b_im, c_re, c_im, d):
    hi = lax.Precision.HIGHEST
    dt = jnp.exp(log_dt)[:, None]
    mag = jnp.exp(a_re * dt)
    lb_re = mag * jnp.cos(a_im * dt)
    lb_im = mag * jnp.sin(a_im * dt)
    nr = lb_re - 1.0
    ni = lb_im
    den = a_re * a_re + a_im * a_im
    coef_re = (nr * a_re + ni * a_im) / den
    coef_im = (ni * a_re - nr * a_im) / den
    bb_re = coef_re[..., None] * b_re - coef_im[..., None] * b_im
    bb_im = coef_re[..., None] * b_im + coef_im[..., None] * b_re

    pw_re = [jnp.ones_like(lb_re)]
    pw_im = [jnp.zeros_like(lb_im)]
    for _ in range(S5_CHUNK):
        pr, pi = pw_re[-1], pw_im[-1]
        pw_re.append(pr * lb_re - pi * lb_im)
        pw_im.append(pr * lb_im + pi * lb_re)
    pw_re = jnp.stack(pw_re)
    pw_im = jnp.stack(pw_im)

    cp_re = c_re[None] * pw_re[:, :, None, :] - c_im[None] * pw_im[:, :, None, :]
    cp_im = c_re[None] * pw_im[:, :, None, :] + c_im[None] * pw_re[:, :, None, :]
    taps = (jnp.einsum("ngcp,gpd->ngcd", cp_re[:S5_CHUNK], bb_re, precision=hi)
            - jnp.einsum("ngcp,gpd->ngcd", cp_im[:S5_CHUNK], bb_im, precision=hi))
    lag = jnp.arange(S5_CHUNK)[None, :] - jnp.arange(S5_CHUNK)[:, None]
    toep = jnp.where((lag >= 0)[:, :, None, None, None], taps[jnp.clip(lag, 0)], 0.0)
    toep = toep.transpose(2, 0, 4, 1, 3)
    eye = jnp.eye(S5_CHUNK, dtype=jnp.float32)
    skip = (eye[None, :, None, :, None] * jnp.eye(S5_GROUP, dtype=jnp.float32)[None, None, :, None, :]
            * d.reshape(S5_GROUPS, 1, 1, 1, S5_GROUP))
    t_mat = (toep + skip).reshape(S5_GROUPS, S5_VEC, S5_VEC)

    wx_re = cp_re[1:].transpose(1, 3, 0, 2).reshape(S5_GROUPS, S5_STATE, S5_VEC)
    wx_im = -cp_im[1:].transpose(1, 3, 0, 2).reshape(S5_GROUPS, S5_STATE, S5_VEC)
    tw = jnp.concatenate([t_mat, wx_re, wx_im], axis=1)

    rev_re = pw_re[S5_CHUNK - 1::-1][:S5_CHUNK]
    rev_im = pw_im[S5_CHUNK - 1::-1][:S5_CHUNK]
    wu_re = rev_re[..., None] * bb_re[None] - rev_im[..., None] * bb_im[None]
    wu_im = rev_re[..., None] * bb_im[None] + rev_im[..., None] * bb_re[None]
    wu = jnp.concatenate([wu_re.transpose(1, 0, 3, 2), wu_im.transpose(1, 0, 3, 2)], axis=-1)
    wu = wu.reshape(S5_GROUPS, S5_VEC, S5_REAL)

    l_re, l_im = pw_re[S5_CHUNK], pw_im[S5_CHUNK]
    ar = jnp.concatenate([l_re, l_re], axis=-1)
    ai = jnp.concatenate([-l_im, l_im], axis=-1)
    coef = jnp.broadcast_to(jnp.stack([ar, ai], axis=1)[:, :, None, :], (S5_GROUPS, 2, BATCH, S5_REAL))
    return wu.astype(jnp.bfloat16), tw.astype(jnp.bfloat16), coef.astype(jnp.float32)


def _pairs_cols(w):
    return w.reshape(w.shape[0], N_PAIRS, PAIR).transpose(1, 0, 2)


def _layer_params(i, g_mix, w_in, conv_a_w, conv_a_b, gate_x_w, gate_x_b, gate_a_w, gate_a_b,
                  lru_lambda, w_a_out, s5_a_re, s5_a_im, s5_log_dt, s5_b_re, s5_b_im, s5_c_re,
                  s5_c_im, s5_d, w_glu, b_glu, w_o, g_ffn, w_up, conv_f_w, conv_f_b, w_down,
                  g_ple, w_ple_gate, w_ple_proj, g_final):
    bf = jnp.bfloat16
    o1, o2 = LRU_WIDTH, 2 * LRU_WIDTH
    o3 = o2 + D_MODEL
    o4 = o3 + D_MODEL
    wi = w_in[i].astype(bf)
    prm = {
        "g_mix": g_mix[i][None], "g_ffn": g_ffn[i][None], "g_ple": g_ple[i][None],
        "g_final": g_final[None],
        "wxa": _pairs_cols(wi[:, :o1]), "wgb": _pairs_cols(wi[:, o1:o2]),
        "wub": wi[:, o2:o3], "wma": wi[:, o3:o4], "wmb": wi[:, o4:],
        "cw": _pairs_cols(conv_a_w[i]), "cb": _pairs_cols(conv_a_b[i][None]),
        "gw": jnp.concatenate([gate_x_w[i], gate_a_w[i]], axis=-1).astype(bf),
        "gxb": _pairs_cols(gate_x_b[i][None]), "gab": _pairs_cols(gate_a_b[i][None]),
        "lam": _pairs_cols(lru_lambda[i][None]),
        "wao": w_a_out[i].astype(bf),
        "wglu": w_glu[i].astype(bf), "bglu": b_glu[i][None], "wo": w_o[i].astype(bf),
        "wup": w_up[i].astype(bf), "cfw": conv_f_w[i], "cfb": conv_f_b[i][None],
        "wdn": w_down[i].astype(bf), "wpg": w_ple_gate[i].astype(bf),
        "wpp": w_ple_proj[i].astype(bf),
    }
    prm["s5_wu"], prm["s5_tw"], prm["s5_coef"] = _s5_matrices(
        s5_a_re[i], s5_a_im[i], s5_log_dt[i], s5_b_re[i], s5_b_im[i], s5_c_re[i], s5_c_im[i],
        s5_d[i])
    return prm


def _to_groups(ub):
    u = ub.reshape(S5_NCHUNK, S5_CHUNK, BATCH, S5_GROUPS, S5_GROUP)
    return u.transpose(3, 0, 2, 1, 4).reshape(S5_GROUPS, S5_ROWS, S5_VEC)


def _from_groups(y):
    y = y.reshape(S5_GROUPS, S5_NCHUNK, BATCH, S5_CHUNK, S5_GROUP)
    return y.transpose(1, 3, 2, 0, 4).reshape(ROWS, D_MODEL)


def kernel(x, p, g_mix, w_in, conv_a_w, conv_a_b, gate_x_w, gate_x_b, gate_a_w, gate_a_b, lru_lambda, w_a_out, s5_a_re, s5_a_im, s5_log_dt, s5_b_re, s5_b_im, s5_c_re, s5_c_im, s5_d, w_glu, b_glu, w_o, g_ffn, w_up, conv_f_w, conv_f_b, w_down, g_ple, w_ple_gate, w_ple_proj, g_final):
    assert x.shape == (BATCH, SEQ, D_MODEL) and p.shape == (DEPTH, BATCH, SEQ, PLE_DIM)
    params = (g_mix, w_in, conv_a_w, conv_a_b, gate_x_w, gate_x_b, gate_a_w, gate_a_b,
              lru_lambda, w_a_out, s5_a_re, s5_a_im, s5_log_dt, s5_b_re, s5_b_im, s5_c_re,
              s5_c_im, s5_d, w_glu, b_glu, w_o, g_ffn, w_up, conv_f_w, conv_f_b, w_down,
              g_ple, w_ple_gate, w_ple_proj, g_final)
    xt = x.transpose(1, 0, 2).reshape(ROWS, D_MODEL)
    pt = p.astype(jnp.bfloat16).transpose(0, 2, 1, 3).reshape(DEPTH, ROWS, PLE_DIM)
    for i in range(DEPTH):
        prm = _layer_params(i, *params)
        ub, ap, sb = _mixer_call(xt, prm)
        y = _from_groups(_s5_call(_to_groups(ub), prm))
        xt = _ffn_call(xt, y, ap, sb, pt[i], prm, final=(i == DEPTH - 1))
    return xt.reshape(SEQ, BATCH, D_MODEL).transpose(1, 0, 2)
```

```python
import functools
import math

import jax
import jax.numpy as jnp
from jax import lax
from jax.experimental import pallas as pl
from jax.experimental.pallas import tpu as pltpu

D_MODEL = 1024
BATCH = 8
SEQ = 4096
DEPTH = 2
PLE_DIM = 256
LRU_WIDTH = 1280
LRU_HEADS = 10
LRU_HEAD_DIM = 128
LRU_CONV = 4
LRU_C = 8.0
S5_GROUP = 16
S5_GROUPS = 64
S5_STATE = 64
D_FF = 3 * D_MODEL
FFN_CONV = 3
EPS = 1e-6

LANES = 128
ROWS = SEQ * BATCH
PAIR = 2 * LRU_HEAD_DIM
N_PAIRS = LRU_WIDTH // PAIR
S5_CHUNK = 16
S5_VEC = S5_CHUNK * S5_GROUP
S5_NCHUNK = SEQ // S5_CHUNK
S5_ROWS = S5_NCHUNK * BATCH
S5_REAL = 2 * S5_STATE
CHUNK_ROWS = S5_CHUNK * BATCH
BLOCKS = LANES // S5_GROUP

O_GB = LRU_WIDTH
O_UB = 2 * LRU_WIDTH
O_MA = O_UB + D_MODEL
O_MB = O_MA + D_MODEL

MIXER_ROWS = 512
FFN_ROWS = 256
FFN_COLS = 512
S5_GROUPS_PER_STEP = 2
VMEM_LIMIT_BYTES = 56 * 1024 * 1024

_GELU_C = math.sqrt(2.0 / math.pi)


def _gelu(x):
    return 0.5 * x * (1.0 + jnp.tanh(_GELU_C * (x + 0.044715 * (x * x * x))))


def _sigmoid(x):
    return 0.5 * jnp.tanh(0.5 * x) + 0.5


def _rmsnorm(x, g):
    ms = jnp.mean(x * x, axis=-1, keepdims=True)
    return x * lax.rsqrt(ms + EPS) * g


def _dot(a, b):
    return jnp.dot(a, b, preferred_element_type=jnp.float32)


def _layer_spec(w, layer):
    zeros = (0,) * (w.ndim - 1)
    return pl.BlockSpec((None,) + w.shape[1:], lambda i: (layer,) + zeros,
                        pipeline_mode=pl.Buffered(1))


def _row_spec(rows, cols):
    return pl.BlockSpec((rows, cols), lambda i: (i, 0))


def _group_spec(rows):
    return pl.BlockSpec((S5_GROUPS, rows // S5_CHUNK, S5_VEC), lambda i: (0, i, 0))


def _block_transpose(v):
    blk = lax.broadcasted_iota(jnp.int32, v[0].shape, 1) // S5_GROUP
    for d in (4, 2, 1):
        bit = (blk & d) != 0
        out = list(v)
        for i in range(BLOCKS):
            if i & d == 0:
                lo, hi = v[i], v[i + d]
                out[i] = jnp.where(bit, pltpu.roll(hi, S5_GROUP * d, axis=1), lo)
                out[i + d] = jnp.where(bit, hi, pltpu.roll(lo, LANES - S5_GROUP * d, axis=1))
        v = out
    return v


def _mixer_kernel(x_ref, g_ref, win_ref, cw_ref, cb_ref, gw_ref, gxb_ref, gab_ref, lam_ref,
                  wao_ref, ug_ref, ap_ref, sb_ref,
                  ext_s, a_s, u_s, gb_s, ma_s, ub_s, hg_s, hc_s):
    rows = x_ref.shape[0]
    nchunk = rows // CHUNK_ROWS
    halo = (LRU_CONV - 1) * BATCH

    @pl.when(pl.program_id(0) == 0)
    def _():
        ext_s[0:halo, :] = jnp.zeros((halo, LRU_WIDTH), jnp.float32)
        hc_s[...] = jnp.zeros_like(hc_s)

    hb = _rmsnorm(x_ref[...], g_ref[...]).astype(jnp.bfloat16)

    lam = lam_ref[...]
    log_sig = jnp.minimum(lam, 0.0) - jnp.log1p(jnp.exp(-jnp.abs(lam)))

    for j in range(N_PAIRS):
        c0 = j * PAIR
        xa = _dot(hb, win_ref[:, c0:c0 + PAIR])
        ext_s[halo:halo + rows, c0:c0 + PAIR] = xa
        xc = cb_ref[:, c0:c0 + PAIR] + cw_ref[3:4, c0:c0 + PAIR] * xa
        for k in range(LRU_CONV - 1):
            xc = xc + cw_ref[k:k + 1, c0:c0 + PAIR] * ext_s[k * BATCH:k * BATCH + rows, c0:c0 + PAIR]
        xcb = xc.astype(jnp.bfloat16)
        for hh in range(2):
            lo = hh * LRU_HEAD_DIM
            cs = slice(c0 + lo, c0 + lo + LRU_HEAD_DIM)
            gates = _dot(xcb[:, lo:lo + LRU_HEAD_DIM], gw_ref[2 * j + hh])
            gx = _sigmoid(gates[:, :LRU_HEAD_DIM] + gxb_ref[:, cs])
            ga = _sigmoid(gates[:, LRU_HEAD_DIM:] + gab_ref[:, cs])
            log_a = (LRU_C * ga) * log_sig[:, cs]
            a = jnp.exp(log_a)
            mult = jnp.sqrt(jnp.maximum((1.0 - a) * (1.0 + a), 1e-12))
            a_s[:, cs] = a
            u_s[:, cs] = mult * (gx * xc[:, lo:lo + LRU_HEAD_DIM])
        gb_s[:, c0:c0 + PAIR] = _gelu(_dot(hb, win_ref[:, O_GB + c0:O_GB + c0 + PAIR]))

    ext_s[0:halo, :] = ext_s[rows:rows + halo, :]

    ma_s[...] = _sigmoid(_dot(hb, win_ref[:, O_MA:O_MA + D_MODEL]))
    sb_ref[...] = _sigmoid(_dot(hb, win_ref[:, O_MB:O_MB + D_MODEL])).astype(sb_ref.dtype)

    ub_s[...] = _dot(hb, win_ref[:, O_UB:O_UB + D_MODEL]).reshape(nchunk, S5_CHUNK, BATCH, D_MODEL)
    for tile in range(D_MODEL // LANES):
        for half in range(S5_CHUNK // BLOCKS):
            v = [ub_s[:, half * BLOCKS + s, :, tile * LANES:(tile + 1) * LANES].reshape(nchunk * BATCH, LANES)
                 for s in range(BLOCKS)]
            w = _block_transpose(v)
            for gi in range(BLOCKS):
                ug_ref[tile * BLOCKS + gi, :, half * LANES:(half + 1) * LANES] = w[gi].astype(ug_ref.dtype)

    h = hc_s[...]
    for t in range(rows // BATCH):
        r = t * BATCH
        h = a_s[r:r + BATCH, :] * h + u_s[r:r + BATCH, :]
        hg_s[r:r + BATCH, :] = h * gb_s[r:r + BATCH, :]
    hc_s[...] = h

    ya = _dot(hg_s[...].astype(jnp.bfloat16), wao_ref[...])
    ap_ref[...] = (ma_s[...] * ya).astype(ap_ref.dtype)


def _mixer_call(xt, layer, prm):
    rows = MIXER_ROWS
    halo = (LRU_CONV - 1) * BATCH
    out = jax.ShapeDtypeStruct((ROWS, D_MODEL), jnp.bfloat16)
    grp = jax.ShapeDtypeStruct((S5_GROUPS, S5_ROWS, S5_VEC), jnp.bfloat16)
    weights = [prm[k] for k in ("g_mix", "w_in", "cw", "cb", "gw", "gxb", "gab", "lam", "wao")]
    f32 = jnp.float32
    return pl.pallas_call(
        _mixer_kernel,
        grid=(ROWS // rows,),
        in_specs=[_row_spec(rows, D_MODEL)] + [_layer_spec(w, layer) for w in weights],
        out_specs=[_group_spec(rows), _row_spec(rows, D_MODEL), _row_spec(rows, D_MODEL)],
        out_shape=[grp, out, out],
        scratch_shapes=[
            pltpu.VMEM((halo + rows, LRU_WIDTH), f32),
            pltpu.VMEM((rows, LRU_WIDTH), f32),
            pltpu.VMEM((rows, LRU_WIDTH), f32),
            pltpu.VMEM((rows, LRU_WIDTH), f32),
            pltpu.VMEM((rows, D_MODEL), f32),
            pltpu.VMEM((rows // CHUNK_ROWS, S5_CHUNK, BATCH, D_MODEL), f32),
            pltpu.VMEM((rows, LRU_WIDTH), f32),
            pltpu.VMEM((BATCH, LRU_WIDTH), f32),
        ],
        compiler_params=pltpu.CompilerParams(
            dimension_semantics=("arbitrary",), vmem_limit_bytes=VMEM_LIMIT_BYTES),
        name="mixer",
    )(xt, *weights)


def _s5_kernel(u_ref, wu_ref, bbt_ref, cp0_ref, cp1_ref, dm_ref, coef_ref, y_ref,
               v_s, vsw_s, xs_s, tw_s):
    ngroups = u_ref.shape[0]
    hi = lax.Precision.HIGHEST
    lane = lax.broadcasted_iota(jnp.int32, (S5_GROUP, S5_VEC), 1)
    for g in range(ngroups):
        taps = (jnp.dot(bbt_ref[g, 0], cp0_ref[g, 0], precision=hi, preferred_element_type=jnp.float32)
                - jnp.dot(bbt_ref[g, 1], cp0_ref[g, 1], precision=hi, preferred_element_type=jnp.float32)
                + dm_ref[g])
        tw_s[g, 0:S5_GROUP, :] = taps.astype(jnp.bfloat16)
        for s in range(1, S5_CHUNK):
            shifted = jnp.where(lane >= s * S5_GROUP, pltpu.roll(taps, s * S5_GROUP, axis=1), 0.0)
            tw_s[g, s * S5_GROUP:(s + 1) * S5_GROUP, :] = shifted.astype(jnp.bfloat16)
        tw_s[g, S5_VEC:S5_VEC + S5_STATE, :] = cp1_ref[g, 0].astype(jnp.bfloat16)
        tw_s[g, S5_VEC + S5_STATE:, :] = (-cp1_ref[g, 1]).astype(jnp.bfloat16)

        v = _dot(u_ref[g], wu_ref[g])
        v_s[g] = v
        vsw_s[g] = pltpu.roll(v, S5_STATE, axis=1)

    def step(k, carry):
        r = pl.multiple_of(k * BATCH, BATCH)
        new = []
        for g in range(ngroups):
            x, xsw = carry[g]
            xs_s[g, pl.ds(r, BATCH), :] = x
            ar = coef_ref[g, 0]
            ai = coef_ref[g, 1]
            xn = ar * x + ai * xsw + v_s[g, pl.ds(r, BATCH), :]
            xswn = ar * xsw - ai * x + vsw_s[g, pl.ds(r, BATCH), :]
            new.append((xn, xswn))
        return tuple(new)

    zero = jnp.zeros((BATCH, S5_REAL), jnp.float32)
    lax.fori_loop(0, S5_NCHUNK, step, tuple((zero, zero) for _ in range(ngroups)), unroll=8)

    for g in range(ngroups):
        y = _dot(u_ref[g], tw_s[g, 0:S5_VEC, :])
        y = y + _dot(xs_s[g].astype(jnp.bfloat16), tw_s[g, S5_VEC:, :])
        y_ref[g] = y.astype(y_ref.dtype)


def _s5_call(u_grp, layer, prm):
    gp = S5_GROUPS_PER_STEP
    weights = [prm[k] for k in ("s5_wu", "s5_bbt", "s5_cp0", "s5_cp1", "s5_dm", "s5_coef")]

    def wspec(w):
        zeros = (0,) * (w.ndim - 2)
        return pl.BlockSpec((None, gp) + w.shape[2:], lambda i: (layer, i) + zeros)

    blk = pl.BlockSpec((gp, S5_ROWS, S5_VEC), lambda i: (i, 0, 0))
    return pl.pallas_call(
        _s5_kernel,
        grid=(S5_GROUPS // gp,),
        in_specs=[blk] + [wspec(w) for w in weights],
        out_specs=blk,
        out_shape=jax.ShapeDtypeStruct((S5_GROUPS, S5_ROWS, S5_VEC), jnp.bfloat16),
        scratch_shapes=[pltpu.VMEM((gp, S5_ROWS, S5_REAL), jnp.float32)] * 3
        + [pltpu.VMEM((gp, S5_VEC + S5_REAL, S5_VEC), jnp.bfloat16)],
        compiler_params=pltpu.CompilerParams(
            dimension_semantics=("arbitrary",), vmem_limit_bytes=VMEM_LIMIT_BYTES),
        name="s5",
    )(u_grp, *weights)


def _ffn_kernel(x_ref, yg_ref, ap_ref, sb_ref, p_ref,
                wglu_ref, bglu_ref, wo_ref, gffn_ref, wup_ref, cfw_ref, cfb_ref, wdn_ref,
                gple_ref, wpg_ref, wpp_ref, gfin_ref,
                o_ref, y_s, ext_s, act_s, fc_s, *, final):
    rows = x_ref.shape[0]
    nchunk = rows // CHUNK_ROWS
    halo = (FFN_CONV - 1) * BATCH

    @pl.when(pl.program_id(0) == 0)
    def _():
        fc_s[...] = jnp.zeros_like(fc_s)

    for tile in range(D_MODEL // LANES):
        for half in range(S5_CHUNK // BLOCKS):
            w = [yg_ref[tile * BLOCKS + gi, :, half * LANES:(half + 1) * LANES].astype(jnp.float32)
                 for gi in range(BLOCKS)]
            v = _block_transpose(w)
            for s in range(BLOCKS):
                y_s[:, half * BLOCKS + s, :, tile * LANES:(tile + 1) * LANES] = (
                    v[s].reshape(nchunk, BATCH, LANES))

    ys = _gelu(y_s[...].reshape(rows, D_MODEL)).astype(jnp.bfloat16)
    glu_a = _dot(ys, wglu_ref[:, 0:D_MODEL]) + bglu_ref[:, 0:D_MODEL]
    glu_b = _dot(ys, wglu_ref[:, D_MODEL:]) + bglu_ref[:, D_MODEL:]
    merged = ap_ref[...].astype(jnp.float32) + sb_ref[...].astype(jnp.float32) * (glu_a * _sigmoid(glu_b))
    x1 = x_ref[...] + _dot(merged.astype(jnp.bfloat16), wo_ref[...])

    h2 = _rmsnorm(x1, gffn_ref[...]).astype(jnp.bfloat16)

    def conv_block(slot, col):
        up = _dot(h2, wup_ref[:, col:col + FFN_COLS])
        ext_s[slot, 0:halo, :] = fc_s[:, col:col + FFN_COLS]
        ext_s[slot, halo:halo + rows, :] = up
        fc_s[:, col:col + FFN_COLS] = up[rows - halo:rows, :]
        out = cfb_ref[:, col:col + FFN_COLS] + cfw_ref[2:3, col:col + FFN_COLS] * up
        for k in range(FFN_CONV - 1):
            out = out + cfw_ref[k:k + 1, col:col + FFN_COLS] * ext_s[slot, k * BATCH:k * BATCH + rows, :]
        return out

    for c in range(D_FF // FFN_COLS):
        col = c * FFN_COLS
        act = _gelu(conv_block(0, col)) * conv_block(1, D_FF + col)
        act_s[:, col:col + FFN_COLS] = act.astype(jnp.bfloat16)

    x2 = x1 + _dot(act_s[...], wdn_ref[...])

    h3 = _rmsnorm(x2, gple_ref[...]).astype(jnp.bfloat16)
    gate = _sigmoid(_dot(h3, wpg_ref[...]))
    x3 = x2 + gate * _dot(p_ref[...], wpp_ref[...])
    if final:
        x3 = _rmsnorm(x3, gfin_ref[...])
    o_ref[...] = x3


def _ffn_call(xt, yg, ap, sb, pt, layer, prm, final):
    rows = FFN_ROWS
    halo = (FFN_CONV - 1) * BATCH
    weights = [prm[k] for k in ("wglu", "bglu", "wo", "g_ffn", "wup", "cfw", "cfb", "wdn",
                                "g_ple", "wpg", "wpp")]
    return pl.pallas_call(
        functools.partial(_ffn_kernel, final=final),
        grid=(ROWS // rows,),
        in_specs=[_row_spec(rows, D_MODEL), _group_spec(rows), _row_spec(rows, D_MODEL),
                  _row_spec(rows, D_MODEL),
                  pl.BlockSpec((None, rows, PLE_DIM), lambda i: (layer, i, 0))]
        + [_layer_spec(w, layer) for w in weights] + [_layer_spec(prm["g_final"], 0)],
        out_specs=_row_spec(rows, D_MODEL),
        out_shape=jax.ShapeDtypeStruct((ROWS, D_MODEL), jnp.float32),
        scratch_shapes=[
            pltpu.VMEM((rows // CHUNK_ROWS, S5_CHUNK, BATCH, D_MODEL), jnp.float32),
            pltpu.VMEM((2, halo + rows, FFN_COLS), jnp.float32),
            pltpu.VMEM((rows, D_FF), jnp.bfloat16),
            pltpu.VMEM((halo, 2 * D_FF), jnp.float32),
        ],
        compiler_params=pltpu.CompilerParams(
            dimension_semantics=("arbitrary",), vmem_limit_bytes=VMEM_LIMIT_BYTES),
        name="ffn_final" if final else "ffn",
    )(xt, yg, ap, sb, pt, *weights, prm["g_final"])


def _s5_params(a_re, a_im, log_dt, b_re, b_im, c_re, c_im, d):
    dt = jnp.exp(log_dt)[..., None]
    lr = a_re * dt
    li = a_im * dt
    mag = jnp.exp(lr)
    lb_re = mag * jnp.cos(li)
    lb_im = mag * jnp.sin(li)
    nr = lb_re - 1.0
    ni = lb_im
    den = a_re * a_re + a_im * a_im
    coef_re = (nr * a_re + ni * a_im) / den
    coef_im = (ni * a_re - nr * a_im) / den
    bb_re = coef_re[..., None] * b_re - coef_im[..., None] * b_im
    bb_im = coef_re[..., None] * b_im + coef_im[..., None] * b_re

    n = jnp.arange(S5_CHUNK + 1, dtype=jnp.float32)
    pmag = jnp.exp(lr[..., None] * n)
    pw_re = pmag * jnp.cos(li[..., None] * n)
    pw_im = pmag * jnp.sin(li[..., None] * n)

    ct_re = jnp.swapaxes(c_re, -1, -2)[..., None, :]
    ct_im = jnp.swapaxes(c_im, -1, -2)[..., None, :]
    cp_re = ct_re * pw_re[..., None] - ct_im * pw_im[..., None]
    cp_im = ct_re * pw_im[..., None] + ct_im * pw_re[..., None]
    lead = cp_re.shape[:3]
    cp0 = jnp.stack([cp_re[..., :S5_CHUNK, :].reshape(lead + (S5_VEC,)),
                     cp_im[..., :S5_CHUNK, :].reshape(lead + (S5_VEC,))], axis=2)
    cp1 = jnp.stack([cp_re[..., 1:, :].reshape(lead + (S5_VEC,)),
                     cp_im[..., 1:, :].reshape(lead + (S5_VEC,))], axis=2)

    bbt_re = jnp.swapaxes(bb_re, -1, -2)
    bbt_im = jnp.swapaxes(bb_im, -1, -2)
    bbt = jnp.stack([bbt_re, bbt_im], axis=2)

    rev_re = jnp.swapaxes(pw_re[..., S5_CHUNK - 1::-1], -1, -2)[..., None, :]
    rev_im = jnp.swapaxes(pw_im[..., S5_CHUNK - 1::-1], -1, -2)[..., None, :]
    wu_re = rev_re * bbt_re[..., None, :, :] - rev_im * bbt_im[..., None, :, :]
    wu_im = rev_re * bbt_im[..., None, :, :] + rev_im * bbt_re[..., None, :, :]
    wu = jnp.concatenate([wu_re, wu_im], axis=-1).reshape(lead[:2] + (S5_VEC, S5_REAL))

    dm = jnp.eye(S5_GROUP, dtype=jnp.float32) * d.reshape(lead[:2] + (1, S5_GROUP))
    dm = jnp.pad(dm, ((0, 0), (0, 0), (0, 0), (0, S5_VEC - S5_GROUP)))

    l_re, l_im = pw_re[..., S5_CHUNK], pw_im[..., S5_CHUNK]
    ar = jnp.concatenate([l_re, l_re], axis=-1)
    ai = jnp.concatenate([-l_im, l_im], axis=-1)
    coef = jnp.broadcast_to(jnp.stack([ar, ai], axis=2)[..., None, :],
                            lead[:2] + (2, BATCH, S5_REAL))
    return {"s5_wu": wu.astype(jnp.bfloat16), "s5_bbt": bbt, "s5_cp0": cp0, "s5_cp1": cp1,
            "s5_dm": dm, "s5_coef": coef}


def kernel(x, p, g_mix, w_in, conv_a_w, conv_a_b, gate_x_w, gate_x_b, gate_a_w, gate_a_b, lru_lambda, w_a_out, s5_a_re, s5_a_im, s5_log_dt, s5_b_re, s5_b_im, s5_c_re, s5_c_im, s5_d, w_glu, b_glu, w_o, g_ffn, w_up, conv_f_w, conv_f_b, w_down, g_ple, w_ple_gate, w_ple_proj, g_final):
    assert x.shape == (BATCH, SEQ, D_MODEL) and p.shape == (DEPTH, BATCH, SEQ, PLE_DIM)
    bf = jnp.bfloat16
    row = lambda v: v[:, None, :]
    prm = {
        "g_mix": row(g_mix), "g_ffn": row(g_ffn), "g_ple": row(g_ple), "g_final": g_final[None, None],
        "w_in": w_in.astype(bf), "cw": conv_a_w, "cb": row(conv_a_b),
        "gw": jnp.concatenate([gate_x_w, gate_a_w], axis=-1).astype(bf),
        "gxb": row(gate_x_b), "gab": row(gate_a_b), "lam": row(lru_lambda),
        "wao": w_a_out.astype(bf),
        "wglu": w_glu.astype(bf), "bglu": row(b_glu), "wo": w_o.astype(bf),
        "wup": w_up.astype(bf), "cfw": conv_f_w, "cfb": row(conv_f_b),
        "wdn": w_down.astype(bf), "wpg": w_ple_gate.astype(bf), "wpp": w_ple_proj.astype(bf),
    }
    prm.update(_s5_params(s5_a_re, s5_a_im, s5_log_dt, s5_b_re, s5_b_im, s5_c_re, s5_c_im, s5_d))

    xt = x.transpose(1, 0, 2).reshape(ROWS, D_MODEL)
    pt = p.astype(bf).transpose(0, 2, 1, 3).reshape(DEPTH, ROWS, PLE_DIM)
    for i in range(DEPTH):
        ug, ap, sb = _mixer_call(xt, i, prm)
        yg = _s5_call(ug, i, prm)
        xt = _ffn_call(xt, yg, ap, sb, pt, i, prm, final=(i == DEPTH - 1))
    return xt.reshape(SEQ, BATCH, D_MODEL).transpose(1, 0, 2)
```

```python
import functools
import math

import jax
import jax.numpy as jnp
from jax import lax
from jax.experimental import pallas as pl
from jax.experimental.pallas import tpu as pltpu

D_MODEL = 1024
BATCH = 8
SEQ = 4096
DEPTH = 2
PLE_DIM = 256
LRU_WIDTH = 1280
LRU_HEADS = 10
LRU_HEAD_DIM = 128
LRU_CONV = 4
LRU_C = 8.0
S5_GROUP = 16
S5_GROUPS = 64
S5_STATE = 64
D_FF = 3 * D_MODEL
FFN_CONV = 3
EPS = 1e-6

LANES = 128
ROWS = SEQ * BATCH
PAIR = 2 * LRU_HEAD_DIM
N_PAIRS = LRU_WIDTH // PAIR
S5_CHUNK = 16
S5_VEC = S5_CHUNK * S5_GROUP
S5_NCHUNK = SEQ // S5_CHUNK
S5_ROWS = S5_NCHUNK * BATCH
S5_REAL = 2 * S5_STATE
CHUNK_ROWS = S5_CHUNK * BATCH
BLOCKS = LANES // S5_GROUP

O_GB = LRU_WIDTH
O_UB = 2 * LRU_WIDTH
O_MA = O_UB + D_MODEL
O_MB = O_MA + D_MODEL

MIXER_ROWS = 512
MERGE_ROWS = 1024
MERGE_SUB = 256
FFN_ROWS = 512
FFN_COLS = 512
S5_GROUPS_PER_STEP = 2
VMEM_LIMIT_BYTES = 56 * 1024 * 1024

_GELU_C = math.sqrt(2.0 / math.pi)


def _gelu2(x):
    return x * (1.0 + jnp.tanh(x * (_GELU_C + (_GELU_C * 0.044715) * (x * x))))


def _sigmoid2(half_x):
    return jnp.tanh(half_x) + 1.0


def _rmsnorm(x, g):
    ms = jnp.mean(x * x, axis=-1, keepdims=True)
    return x * lax.rsqrt(ms + EPS) * g


def _dot(a, b):
    return jnp.dot(a, b, preferred_element_type=jnp.float32)


def _layer_spec(w, layer):
    zeros = (0,) * (w.ndim - 1)
    return pl.BlockSpec((None,) + w.shape[1:], lambda i: (layer,) + zeros,
                        pipeline_mode=pl.Buffered(1))


def _row_spec(rows, cols):
    return pl.BlockSpec((rows, cols), lambda i: (i, 0))


def _group_spec(rows):
    return pl.BlockSpec((S5_GROUPS, rows // S5_CHUNK, S5_VEC), lambda i: (0, i, 0))


def _block_transpose(v):
    blk = lax.broadcasted_iota(jnp.int32, v[0].shape, 1) // S5_GROUP
    for d in (4, 2, 1):
        bit = (blk & d) != 0
        out = list(v)
        for i in range(BLOCKS):
            if i & d == 0:
                lo, hi = v[i], v[i + d]
                out[i] = jnp.where(bit, pltpu.roll(hi, S5_GROUP * d, axis=1), lo)
                out[i + d] = jnp.where(bit, hi, pltpu.roll(lo, LANES - S5_GROUP * d, axis=1))
        v = out
    return v


def _mixer_kernel(x_ref, g_ref, win_ref, cw_ref, cb_ref, gw_ref, gxb_ref, gab_ref, lam_ref,
                  wao_ref, ug_ref, ap_ref, sb_ref,
                  ext_s, a_s, u_s, gb_s, ma_s, ub_s, hg_s, hc_s):
    rows = x_ref.shape[0]
    nchunk = rows // CHUNK_ROWS
    halo = (LRU_CONV - 1) * BATCH

    @pl.when(pl.program_id(0) == 0)
    def _():
        ext_s[0:halo, :] = jnp.zeros((halo, LRU_WIDTH), jnp.float32)
        hc_s[...] = jnp.zeros_like(hc_s)

    hb = _rmsnorm(x_ref[...], g_ref[...]).astype(jnp.bfloat16)

    lam = lam_ref[...]
    log_coef = (0.5 * LRU_C) * (jnp.minimum(lam, 0.0) - jnp.log1p(jnp.exp(-jnp.abs(lam))))

    for j in range(N_PAIRS):
        c0 = j * PAIR
        xa = _dot(hb, win_ref[:, c0:c0 + PAIR])
        ext_s[halo:halo + rows, c0:c0 + PAIR] = xa
        xc = cb_ref[:, c0:c0 + PAIR] + cw_ref[3:4, c0:c0 + PAIR] * xa
        for k in range(LRU_CONV - 1):
            xc = xc + cw_ref[k:k + 1, c0:c0 + PAIR] * ext_s[k * BATCH:k * BATCH + rows, c0:c0 + PAIR]
        xcb = xc.astype(jnp.bfloat16)
        for hh in range(2):
            lo = hh * LRU_HEAD_DIM
            cs = slice(c0 + lo, c0 + lo + LRU_HEAD_DIM)
            gates = _dot(xcb[:, lo:lo + LRU_HEAD_DIM], gw_ref[2 * j + hh])
            gx2 = _sigmoid2(gates[:, :LRU_HEAD_DIM] + gxb_ref[:, cs])
            log_a = log_coef[:, cs] * jnp.tanh(gates[:, LRU_HEAD_DIM:] + gab_ref[:, cs]) + log_coef[:, cs]
            a = jnp.exp(log_a)
            mult = jnp.sqrt(jnp.maximum((1.0 - a) * (1.0 + a), 1e-12))
            a_s[:, cs] = a
            u_s[:, cs] = mult * (gx2 * xc[:, lo:lo + LRU_HEAD_DIM])
        gb_s[:, c0:c0 + PAIR] = _gelu2(_dot(hb, win_ref[:, O_GB + c0:O_GB + c0 + PAIR]))

    ext_s[0:halo, :] = ext_s[rows:rows + halo, :]

    ma_s[...] = _sigmoid2(_dot(hb, win_ref[:, O_MA:O_MA + D_MODEL]))
    sb_ref[...] = _sigmoid2(_dot(hb, win_ref[:, O_MB:O_MB + D_MODEL])).astype(sb_ref.dtype)

    ub_s[...] = _dot(hb, win_ref[:, O_UB:O_UB + D_MODEL]).reshape(nchunk, S5_CHUNK, BATCH, D_MODEL)
    for tile in range(D_MODEL // LANES):
        for half in range(S5_CHUNK // BLOCKS):
            v = [ub_s[:, half * BLOCKS + s, :, tile * LANES:(tile + 1) * LANES].reshape(nchunk * BATCH, LANES)
                 for s in range(BLOCKS)]
            w = _block_transpose(v)
            for gi in range(BLOCKS):
                ug_ref[tile * BLOCKS + gi, :, half * LANES:(half + 1) * LANES] = w[gi].astype(ug_ref.dtype)

    h = hc_s[...]
    for t in range(rows // BATCH):
        r = t * BATCH
        h = a_s[r:r + BATCH, :] * h + u_s[r:r + BATCH, :]
        hg_s[r:r + BATCH, :] = h * gb_s[r:r + BATCH, :]
    hc_s[...] = h

    ya = _dot(hg_s[...].astype(jnp.bfloat16), wao_ref[...])
    ap_ref[...] = (ma_s[...] * ya).astype(ap_ref.dtype)


def _mixer_call(xt, layer, prm):
    rows = MIXER_ROWS
    halo = (LRU_CONV - 1) * BATCH
    out = jax.ShapeDtypeStruct((ROWS, D_MODEL), jnp.bfloat16)
    grp = jax.ShapeDtypeStruct((S5_GROUPS, S5_ROWS, S5_VEC), jnp.bfloat16)
    weights = [prm[k] for k in ("g_mix", "w_in", "cw", "cb", "gw", "gxb", "gab", "lam", "wao")]
    f32 = jnp.float32
    return pl.pallas_call(
        _mixer_kernel,
        grid=(ROWS // rows,),
        in_specs=[_row_spec(rows, D_MODEL)] + [_layer_spec(w, layer) for w in weights],
        out_specs=[_group_spec(rows), _row_spec(rows, D_MODEL), _row_spec(rows, D_MODEL)],
        out_shape=[grp, out, out],
        scratch_shapes=[
            pltpu.VMEM((halo + rows, LRU_WIDTH), f32),
            pltpu.VMEM((rows, LRU_WIDTH), f32),
            pltpu.VMEM((rows, LRU_WIDTH), f32),
            pltpu.VMEM((rows, LRU_WIDTH), f32),
            pltpu.VMEM((rows, D_MODEL), f32),
            pltpu.VMEM((rows // CHUNK_ROWS, S5_CHUNK, BATCH, D_MODEL), f32),
            pltpu.VMEM((rows, LRU_WIDTH), f32),
            pltpu.VMEM((BATCH, LRU_WIDTH), f32),
        ],
        compiler_params=pltpu.CompilerParams(
            dimension_semantics=("arbitrary",), vmem_limit_bytes=VMEM_LIMIT_BYTES),
        name="mixer",
    )(xt, *weights)


def _s5_kernel(u_ref, wu_ref, bbt_ref, cp0_ref, cp1_ref, dm_ref, coef_ref, y_ref,
               v_s, vsw_s, xs_s, tw_s):
    ngroups = u_ref.shape[0]
    hi = lax.Precision.HIGHEST
    lane = lax.broadcasted_iota(jnp.int32, (S5_GROUP, S5_VEC), 1)
    for g in range(ngroups):
        taps = (jnp.dot(bbt_ref[g, 0], cp0_ref[g, 0], precision=hi, preferred_element_type=jnp.float32)
                - jnp.dot(bbt_ref[g, 1], cp0_ref[g, 1], precision=hi, preferred_element_type=jnp.float32)
                + dm_ref[g])
        tw_s[g, 0:S5_GROUP, :] = taps.astype(jnp.bfloat16)
        for s in range(1, S5_CHUNK):
            shifted = jnp.where(lane >= s * S5_GROUP, pltpu.roll(taps, s * S5_GROUP, axis=1), 0.0)
            tw_s[g, s * S5_GROUP:(s + 1) * S5_GROUP, :] = shifted.astype(jnp.bfloat16)
        tw_s[g, S5_VEC:S5_VEC + S5_STATE, :] = cp1_ref[g, 0].astype(jnp.bfloat16)
        tw_s[g, S5_VEC + S5_STATE:, :] = (-cp1_ref[g, 1]).astype(jnp.bfloat16)

        v = _dot(u_ref[g], wu_ref[g])
        v_s[g] = v
        vsw_s[g] = pltpu.roll(v, S5_STATE, axis=1)

    def step(k, carry):
        r = pl.multiple_of(k * BATCH, BATCH)
        new = []
        for g in range(ngroups):
            x, xsw = carry[g]
            xs_s[g, pl.ds(r, BATCH), :] = x
            ar = coef_ref[g, 0]
            ai = coef_ref[g, 1]
            xn = ar * x + ai * xsw + v_s[g, pl.ds(r, BATCH), :]
            xswn = ar * xsw - ai * x + vsw_s[g, pl.ds(r, BATCH), :]
            new.append((xn, xswn))
        return tuple(new)

    zero = jnp.zeros((BATCH, S5_REAL), jnp.float32)
    lax.fori_loop(0, S5_NCHUNK, step, tuple((zero, zero) for _ in range(ngroups)), unroll=8)

    for g in range(ngroups):
        y = _dot(u_ref[g], tw_s[g, 0:S5_VEC, :])
        y = y + _dot(xs_s[g].astype(jnp.bfloat16), tw_s[g, S5_VEC:, :])
        y_ref[g] = y.astype(y_ref.dtype)


def _s5_call(u_grp, layer, prm):
    gp = S5_GROUPS_PER_STEP
    weights = [prm[k] for k in ("s5_wu", "s5_bbt", "s5_cp0", "s5_cp1", "s5_dm", "s5_coef")]

    def wspec(w):
        zeros = (0,) * (w.ndim - 2)
        return pl.BlockSpec((None, gp) + w.shape[2:], lambda i: (layer, i) + zeros)

    blk = pl.BlockSpec((gp, S5_ROWS, S5_VEC), lambda i: (i, 0, 0))
    return pl.pallas_call(
        _s5_kernel,
        grid=(S5_GROUPS // gp,),
        in_specs=[blk] + [wspec(w) for w in weights],
        out_specs=blk,
        out_shape=jax.ShapeDtypeStruct((S5_GROUPS, S5_ROWS, S5_VEC), jnp.bfloat16),
        scratch_shapes=[pltpu.VMEM((gp, S5_ROWS, S5_REAL), jnp.float32)] * 3
        + [pltpu.VMEM((gp, S5_VEC + S5_REAL, S5_VEC), jnp.bfloat16)],
        compiler_params=pltpu.CompilerParams(
            dimension_semantics=("arbitrary",), vmem_limit_bytes=VMEM_LIMIT_BYTES),
        name="s5",
    )(u_grp, *weights)


def _merge_kernel(x_ref, yg_ref, ap_ref, sb_ref, wglu_ref, bglu_ref, wo_ref, o_ref, y_s):
    nchunk = MERGE_SUB // CHUNK_ROWS
    grows = MERGE_SUB // S5_CHUNK
    for sub in range(x_ref.shape[0] // MERGE_SUB):
        rs = slice(sub * MERGE_SUB, (sub + 1) * MERGE_SUB)
        gs = slice(sub * grows, (sub + 1) * grows)
        cs = slice(sub * nchunk, (sub + 1) * nchunk)
        for tile in range(D_MODEL // LANES):
            for half in range(S5_CHUNK // BLOCKS):
                w = [yg_ref[tile * BLOCKS + gi, gs, half * LANES:(half + 1) * LANES].astype(jnp.float32)
                     for gi in range(BLOCKS)]
                v = _block_transpose(w)
                for s in range(BLOCKS):
                    y_s[cs, half * BLOCKS + s, :, tile * LANES:(tile + 1) * LANES] = (
                        v[s].reshape(nchunk, BATCH, LANES))

        ys = _gelu2(y_s[cs].reshape(MERGE_SUB, D_MODEL)).astype(jnp.bfloat16)
        glu_a = _dot(ys, wglu_ref[:, 0:D_MODEL]) + bglu_ref[:, 0:D_MODEL]
        glu_b = _dot(ys, wglu_ref[:, D_MODEL:]) + bglu_ref[:, D_MODEL:]
        merged = (ap_ref[rs, :].astype(jnp.float32)
                  + sb_ref[rs, :].astype(jnp.float32) * (glu_a * _sigmoid2(glu_b)))
        o_ref[rs, :] = x_ref[rs, :] + _dot(merged.astype(jnp.bfloat16), wo_ref[...])


def _merge_call(xt, yg, ap, sb, layer, prm):
    rows = MERGE_ROWS
    weights = [prm[k] for k in ("wglu", "bglu", "wo")]
    return pl.pallas_call(
        _merge_kernel,
        grid=(ROWS // rows,),
        in_specs=[_row_spec(rows, D_MODEL), _group_spec(rows), _row_spec(rows, D_MODEL),
                  _row_spec(rows, D_MODEL)] + [_layer_spec(w, layer) for w in weights],
        out_specs=_row_spec(rows, D_MODEL),
        out_shape=jax.ShapeDtypeStruct((ROWS, D_MODEL), jnp.float32),
        scratch_shapes=[pltpu.VMEM((rows // CHUNK_ROWS, S5_CHUNK, BATCH, D_MODEL), jnp.float32)],
        compiler_params=pltpu.CompilerParams(
            dimension_semantics=("arbitrary",), vmem_limit_bytes=VMEM_LIMIT_BYTES),
        name="merge",
    )(xt, yg, ap, sb, *weights)


def _ffn_kernel(x_ref, p_ref, gffn_ref, wup_ref, cfw_ref, cfb_ref, wdn_ref,
                gple_ref, wpg_ref, wpp_ref, gfin_ref,
                o_ref, ext_s, act_s, fc_s, *, final):
    rows = x_ref.shape[0]
    halo = (FFN_CONV - 1) * BATCH

    @pl.when(pl.program_id(0) == 0)
    def _():
        fc_s[...] = jnp.zeros_like(fc_s)

    x1 = x_ref[...]
    h2 = _rmsnorm(x1, gffn_ref[...]).astype(jnp.bfloat16)

    def conv_block(slot, col):
        up = _dot(h2, wup_ref[:, col:col + FFN_COLS])
        ext_s[slot, 0:halo, :] = fc_s[:, col:col + FFN_COLS]
        ext_s[slot, halo:halo + rows, :] = up
        fc_s[:, col:col + FFN_COLS] = up[rows - halo:rows, :]
        out = cfb_ref[:, col:col + FFN_COLS] + cfw_ref[2:3, col:col + FFN_COLS] * up
        for k in range(FFN_CONV - 1):
            out = out + cfw_ref[k:k + 1, col:col + FFN_COLS] * ext_s[slot, k * BATCH:k * BATCH + rows, :]
        return out

    for c in range(D_FF // FFN_COLS):
        col = c * FFN_COLS
        slot = 2 * (c % 2)
        act = _gelu2(conv_block(slot, col)) * conv_block(slot + 1, D_FF + col)
        act_s[:, col:col + FFN_COLS] = act.astype(jnp.bfloat16)

    x2 = x1 + _dot(act_s[...], wdn_ref[...])

    h3 = _rmsnorm(x2, gple_ref[...]).astype(jnp.bfloat16)
    gate = _sigmoid2(_dot(h3, wpg_ref[...]))
    x3 = x2 + gate * _dot(p_ref[...], wpp_ref[...])
    if final:
        x3 = _rmsnorm(x3, gfin_ref[...])
    o_ref[...] = x3


def _ffn_call(x1, pt, layer, prm, final):
    rows = FFN_ROWS
    halo = (FFN_CONV - 1) * BATCH
    weights = [prm[k] for k in ("g_ffn", "wup", "cfw", "cfb", "wdn", "g_ple", "wpg", "wpp")]
    return pl.pallas_call(
        functools.partial(_ffn_kernel, final=final),
        grid=(ROWS // rows,),
        in_specs=[_row_spec(rows, D_MODEL),
                  pl.BlockSpec((None, rows, PLE_DIM), lambda i: (layer, i, 0))]
        + [_layer_spec(w, layer) for w in weights] + [_layer_spec(prm["g_final"], 0)],
        out_specs=_row_spec(rows, D_MODEL),
        out_shape=jax.ShapeDtypeStruct((ROWS, D_MODEL), jnp.float32),
        scratch_shapes=[
            pltpu.VMEM((4, halo + rows, FFN_COLS), jnp.float32),
            pltpu.VMEM((rows, D_FF), jnp.bfloat16),
            pltpu.VMEM((halo, 2 * D_FF), jnp.float32),
        ],
        compiler_params=pltpu.CompilerParams(
            dimension_semantics=("arbitrary",), vmem_limit_bytes=VMEM_LIMIT_BYTES),
        name="ffn_final" if final else "ffn",
    )(x1, pt, *weights, prm["g_final"])


def _s5_params(a_re, a_im, log_dt, b_re, b_im, c_re, c_im, d):
    dt = jnp.exp(log_dt)[..., None]
    lr = a_re * dt
    li = a_im * dt
    mag = jnp.exp(lr)
    lb_re = mag * jnp.cos(li)
    lb_im = mag * jnp.sin(li)
    nr = lb_re - 1.0
    ni = lb_im
    den = a_re * a_re + a_im * a_im
    coef_re = (nr * a_re + ni * a_im) / den
    coef_im = (ni * a_re - nr * a_im) / den
    bb_re = coef_re[..., None] * b_re - coef_im[..., None] * b_im
    bb_im = coef_re[..., None] * b_im + coef_im[..., None] * b_re

    n = jnp.arange(S5_CHUNK + 1, dtype=jnp.float32)
    pmag = jnp.exp(lr[..., None] * n)
    pw_re = pmag * jnp.cos(li[..., None] * n)
    pw_im = pmag * jnp.sin(li[..., None] * n)

    ct_re = jnp.swapaxes(c_re, -1, -2)[..., None, :]
    ct_im = jnp.swapaxes(c_im, -1, -2)[..., None, :]
    cp_re = ct_re * pw_re[..., None] - ct_im * pw_im[..., None]
    cp_im = ct_re * pw_im[..., None] + ct_im * pw_re[..., None]
    lead = cp_re.shape[:3]
    cp0 = jnp.stack([cp_re[..., :S5_CHUNK, :].reshape(lead + (S5_VEC,)),
                     cp_im[..., :S5_CHUNK, :].reshape(lead + (S5_VEC,))], axis=2)
    cp1 = jnp.stack([cp_re[..., 1:, :].reshape(lead + (S5_VEC,)),
                     cp_im[..., 1:, :].reshape(lead + (S5_VEC,))], axis=2)

    bbt_re = jnp.swapaxes(bb_re, -1, -2)
    bbt_im = jnp.swapaxes(bb_im, -1, -2)
    bbt = jnp.stack([bbt_re, bbt_im], axis=2)

    rev_re = jnp.swapaxes(pw_re[..., S5_CHUNK - 1::-1], -1, -2)[..., None, :]
    rev_im = jnp.swapaxes(pw_im[..., S5_CHUNK - 1::-1], -1, -2)[..., None, :]
    wu_re = rev_re * bbt_re[..., None, :, :] - rev_im * bbt_im[..., None, :, :]
    wu_im = rev_re * bbt_im[..., None, :, :] + rev_im * bbt_re[..., None, :, :]
    wu = jnp.concatenate([wu_re, wu_im], axis=-1).reshape(lead[:2] + (S5_VEC, S5_REAL))

    dm = jnp.eye(S5_GROUP, dtype=jnp.float32) * d.reshape(lead[:2] + (1, S5_GROUP))
    dm = jnp.pad(dm, ((0, 0), (0, 0), (0, 0), (0, S5_VEC - S5_GROUP)))

    l_re, l_im = pw_re[..., S5_CHUNK], pw_im[..., S5_CHUNK]
    ar = jnp.concatenate([l_re, l_re], axis=-1)
    ai = jnp.concatenate([-l_im, l_im], axis=-1)
    coef = jnp.broadcast_to(jnp.stack([ar, ai], axis=2)[..., None, :],
                            lead[:2] + (2, BATCH, S5_REAL))
    return {"s5_wu": wu.astype(jnp.bfloat16), "s5_bbt": bbt, "s5_cp0": cp0, "s5_cp1": cp1,
            "s5_dm": dm, "s5_coef": coef}


def kernel(x, p, g_mix, w_in, conv_a_w, conv_a_b, gate_x_w, gate_x_b, gate_a_w, gate_a_b, lru_lambda, w_a_out, s5_a_re, s5_a_im, s5_log_dt, s5_b_re, s5_b_im, s5_c_re, s5_c_im, s5_d, w_glu, b_glu, w_o, g_ffn, w_up, conv_f_w, conv_f_b, w_down, g_ple, w_ple_gate, w_ple_proj, g_final):
    assert x.shape == (BATCH, SEQ, D_MODEL) and p.shape == (DEPTH, BATCH, SEQ, PLE_DIM)
    bf = jnp.bfloat16
    row = lambda v: v[:, None, :]
    ones = functools.partial(jnp.full, dtype=jnp.float32)
    in_scale = jnp.concatenate([ones((O_MA,), 1.0), ones((2 * D_MODEL,), 0.5)])
    glu_scale = jnp.concatenate([ones((D_MODEL,), 0.125), ones((D_MODEL,), 0.25)])
    prm = {
        "g_mix": row(g_mix), "g_ffn": row(g_ffn), "g_ple": row(g_ple), "g_final": g_final[None, None],
        "w_in": (w_in * in_scale).astype(bf), "cw": conv_a_w, "cb": row(conv_a_b),
        "gw": (0.5 * jnp.concatenate([gate_x_w, gate_a_w], axis=-1)).astype(bf),
        "gxb": row(0.5 * gate_x_b), "gab": row(0.5 * gate_a_b), "lam": row(lru_lambda),
        "wao": (0.125 * w_a_out).astype(bf),
        "wglu": (w_glu * glu_scale).astype(bf), "bglu": row(b_glu * (2.0 * glu_scale)),
        "wo": w_o.astype(bf),
        "wup": w_up.astype(bf), "cfw": conv_f_w, "cfb": row(conv_f_b),
        "wdn": (0.5 * w_down).astype(bf), "wpg": (0.5 * w_ple_gate).astype(bf),
        "wpp": (0.5 * w_ple_proj).astype(bf),
    }
    prm.update(_s5_params(s5_a_re, s5_a_im, s5_log_dt, s5_b_re, s5_b_im, s5_c_re, s5_c_im, s5_d))

    xt = x.transpose(1, 0, 2).reshape(ROWS, D_MODEL)
    pt = p.astype(bf).transpose(0, 2, 1, 3).reshape(DEPTH, ROWS, PLE_DIM)
    for i in range(DEPTH):
        ug, ap, sb = _mixer_call(xt, i, prm)
        yg = _s5_call(ug, i, prm)
        x1 = _merge_call(xt, yg, ap, sb, i, prm)
        xt = _ffn_call(x1, pt, i, prm, final=(i == DEPTH - 1))
    return xt.reshape(SEQ, BATCH, D_MODEL).transpose(1, 0, 2)
```

```python
import functools
import math

import jax
import jax.numpy as jnp
from jax import lax
from jax.experimental import pallas as pl
from jax.experimental.pallas import tpu as pltpu

D_MODEL = 1024
BATCH = 8
SEQ = 4096
DEPTH = 2
PLE_DIM = 256
LRU_WIDTH = 1280
LRU_HEADS = 10
LRU_HEAD_DIM = 128
LRU_CONV = 4
LRU_C = 8.0
S5_GROUP = 16
S5_GROUPS = 64
S5_STATE = 64
D_FF = 3 * D_MODEL
FFN_CONV = 3
EPS = 1e-6

LANES = 128
ROWS = SEQ * BATCH
PAIR = 2 * LRU_HEAD_DIM
N_PAIRS = LRU_WIDTH // PAIR
S5_CHUNK = 16
S5_VEC = S5_CHUNK * S5_GROUP
S5_NCHUNK = SEQ // S5_CHUNK
S5_ROWS = S5_NCHUNK * BATCH
S5_REAL = 2 * S5_STATE
CHUNK_ROWS = S5_CHUNK * BATCH
BLOCKS = LANES // S5_GROUP

O_GB = LRU_WIDTH
O_UB = 2 * LRU_WIDTH
O_MA = O_UB + D_MODEL
O_MB = O_MA + D_MODEL

MIXER_ROWS = 512
MERGE_ROWS = 1024
MERGE_SUB = 256
FFN_ROWS = 512
FFN_COLS = 512
S5_GROUPS_PER_STEP = 4
VMEM_LIMIT_BYTES = 56 * 1024 * 1024

_GELU_C = math.sqrt(2.0 / math.pi)


def _gelu2(x):
    return x * (1.0 + jnp.tanh(x * (_GELU_C + (_GELU_C * 0.044715) * (x * x))))


def _sigmoid2(half_x):
    return jnp.tanh(half_x) + 1.0


def _rmsnorm(x, g):
    ms = jnp.mean(x * x, axis=-1, keepdims=True)
    return x * lax.rsqrt(ms + EPS) * g


def _dot(a, b):
    return jnp.dot(a, b, preferred_element_type=jnp.float32)


def _layer_spec(w, layer):
    zeros = (0,) * (w.ndim - 1)
    return pl.BlockSpec((None,) + w.shape[1:], lambda i: (layer,) + zeros,
                        pipeline_mode=pl.Buffered(1))


def _row_spec(rows, cols):
    return pl.BlockSpec((rows, cols), lambda i: (i, 0))


def _batch_major_spec(rows, cols):
    return pl.BlockSpec((BATCH, rows // BATCH, cols), lambda i: (0, i, 0))


def _to_time_major(src_ref, dst_ref):
    for t in range(src_ref.shape[1]):
        dst_ref[t * BATCH:(t + 1) * BATCH, :] = src_ref[:, t, :]


def _group_spec(rows):
    return pl.BlockSpec((S5_GROUPS, rows // S5_CHUNK, S5_VEC), lambda i: (0, i, 0))


def _block_transpose(v):
    blk = lax.broadcasted_iota(jnp.int32, v[0].shape, 1) // S5_GROUP
    for d in (4, 2, 1):
        bit = (blk & d) != 0
        out = list(v)
        for i in range(BLOCKS):
            if i & d == 0:
                lo, hi = v[i], v[i + d]
                out[i] = jnp.where(bit, pltpu.roll(hi, S5_GROUP * d, axis=1), lo)
                out[i + d] = jnp.where(bit, hi, pltpu.roll(lo, LANES - S5_GROUP * d, axis=1))
        v = out
    return v


def _mixer_kernel(x_ref, g_ref, win_ref, cw_ref, cb_ref, gw_ref, gxb_ref, gab_ref, lam_ref,
                  wao_ref, ug_ref, ap_ref, sb_ref, *rest, batch_major_in):
    if batch_major_in:
        xt_ref, ext_s, a_s, u_s, gb_s, ma_s, ub_s, hg_s, hc_s = rest
        _to_time_major(x_ref, xt_ref)
        x_ref = xt_ref
    else:
        ext_s, a_s, u_s, gb_s, ma_s, ub_s, hg_s, hc_s = rest
    rows = x_ref.shape[0]
    nchunk = rows // CHUNK_ROWS
    halo = (LRU_CONV - 1) * BATCH

    @pl.when(pl.program_id(0) == 0)
    def _():
        ext_s[0:halo, :] = jnp.zeros((halo, LRU_WIDTH), jnp.float32)
        hc_s[...] = jnp.zeros_like(hc_s)

    hb = _rmsnorm(x_ref[...], g_ref[...]).astype(jnp.bfloat16)

    lam = lam_ref[...]
    log_coef = (0.5 * LRU_C) * (jnp.minimum(lam, 0.0) - jnp.log1p(jnp.exp(-jnp.abs(lam))))

    for j in range(N_PAIRS):
        c0 = j * PAIR
        xa = _dot(hb, win_ref[:, c0:c0 + PAIR])
        ext_s[halo:halo + rows, c0:c0 + PAIR] = xa
        xc = cb_ref[:, c0:c0 + PAIR] + cw_ref[3:4, c0:c0 + PAIR] * xa
        for k in range(LRU_CONV - 1):
            xc = xc + cw_ref[k:k + 1, c0:c0 + PAIR] * ext_s[k * BATCH:k * BATCH + rows, c0:c0 + PAIR]
        xcb = xc.astype(jnp.bfloat16)
        for hh in range(2):
            lo = hh * LRU_HEAD_DIM
            cs = slice(c0 + lo, c0 + lo + LRU_HEAD_DIM)
            gates = _dot(xcb[:, lo:lo + LRU_HEAD_DIM], gw_ref[2 * j + hh])
            gx2 = _sigmoid2(gates[:, :LRU_HEAD_DIM] + gxb_ref[:, cs])
            log_a = log_coef[:, cs] * jnp.tanh(gates[:, LRU_HEAD_DIM:] + gab_ref[:, cs]) + log_coef[:, cs]
            a = jnp.exp(log_a)
            mult = jnp.sqrt(jnp.maximum((1.0 - a) * (1.0 + a), 1e-12))
            a_s[:, cs] = a
            u_s[:, cs] = mult * (gx2 * xc[:, lo:lo + LRU_HEAD_DIM])
        gb_s[:, c0:c0 + PAIR] = _gelu2(_dot(hb, win_ref[:, O_GB + c0:O_GB + c0 + PAIR]))

    ext_s[0:halo, :] = ext_s[rows:rows + halo, :]

    ma_s[...] = _sigmoid2(_dot(hb, win_ref[:, O_MA:O_MA + D_MODEL]))
    sb_ref[...] = _sigmoid2(_dot(hb, win_ref[:, O_MB:O_MB + D_MODEL])).astype(sb_ref.dtype)

    ub_s[...] = _dot(hb, win_ref[:, O_UB:O_UB + D_MODEL]).reshape(nchunk, S5_CHUNK, BATCH, D_MODEL)
    for tile in range(D_MODEL // LANES):
        for half in range(S5_CHUNK // BLOCKS):
            v = [ub_s[:, half * BLOCKS + s, :, tile * LANES:(tile + 1) * LANES].reshape(nchunk * BATCH, LANES)
                 for s in range(BLOCKS)]
            w = _block_transpose(v)
            for gi in range(BLOCKS):
                ug_ref[tile * BLOCKS + gi, :, half * LANES:(half + 1) * LANES] = w[gi].astype(ug_ref.dtype)

    h = hc_s[...]
    for t in range(rows // BATCH):
        r = t * BATCH
        h = a_s[r:r + BATCH, :] * h + u_s[r:r + BATCH, :]
        hg_s[r:r + BATCH, :] = h * gb_s[r:r + BATCH, :]
    hc_s[...] = h

    ya = _dot(hg_s[...].astype(jnp.bfloat16), wao_ref[...])
    ap_ref[...] = (ma_s[...] * ya).astype(ap_ref.dtype)


def _mixer_call(x, layer, prm, batch_major_in):
    rows = MIXER_ROWS
    halo = (LRU_CONV - 1) * BATCH
    out = jax.ShapeDtypeStruct((ROWS, D_MODEL), jnp.bfloat16)
    grp = jax.ShapeDtypeStruct((S5_GROUPS, S5_ROWS, S5_VEC), jnp.bfloat16)
    weights = [prm[k] for k in ("g_mix", "w_in", "cw", "cb", "gw", "gxb", "gab", "lam", "wao")]
    f32 = jnp.float32
    x_spec = _batch_major_spec(rows, D_MODEL) if batch_major_in else _row_spec(rows, D_MODEL)
    extra = batch_major_in * [jax.ShapeDtypeStruct((ROWS, D_MODEL), f32)]
    return pl.pallas_call(
        functools.partial(_mixer_kernel, batch_major_in=batch_major_in),
        grid=(ROWS // rows,),
        in_specs=[x_spec] + [_layer_spec(w, layer) for w in weights],
        out_specs=[_group_spec(rows)] + [_row_spec(rows, D_MODEL)] * (2 + len(extra)),
        out_shape=[grp, out, out] + extra,
        scratch_shapes=[
            pltpu.VMEM((halo + rows, LRU_WIDTH), f32),
            pltpu.VMEM((rows, LRU_WIDTH), f32),
            pltpu.VMEM((rows, LRU_WIDTH), f32),
            pltpu.VMEM((rows, LRU_WIDTH), f32),
            pltpu.VMEM((rows, D_MODEL), f32),
            pltpu.VMEM((rows // CHUNK_ROWS, S5_CHUNK, BATCH, D_MODEL), f32),
            pltpu.VMEM((rows, LRU_WIDTH), f32),
            pltpu.VMEM((BATCH, LRU_WIDTH), f32),
        ],
        compiler_params=pltpu.CompilerParams(
            dimension_semantics=("arbitrary",), vmem_limit_bytes=VMEM_LIMIT_BYTES),
        name="mixer_in" if batch_major_in else "mixer",
    )(x, *weights)


def _s5_kernel(u_ref, wu_ref, bbt_ref, cp0_ref, cp1_ref, dm_ref, coef_ref, y_ref,
               v_s, vsw_s, xs_s, tw_s):
    ngroups = u_ref.shape[0]
    hi = lax.Precision.HIGHEST
    lane = lax.broadcasted_iota(jnp.int32, (S5_GROUP, S5_VEC), 1)
    for g in range(ngroups):
        taps = (jnp.dot(bbt_ref[g, 0], cp0_ref[g, 0], precision=hi, preferred_element_type=jnp.float32)
                - jnp.dot(bbt_ref[g, 1], cp0_ref[g, 1], precision=hi, preferred_element_type=jnp.float32)
                + dm_ref[g])
        tw_s[g, 0:S5_GROUP, :] = taps.astype(jnp.bfloat16)
        for s in range(1, S5_CHUNK):
            shifted = jnp.where(lane >= s * S5_GROUP, pltpu.roll(taps, s * S5_GROUP, axis=1), 0.0)
            tw_s[g, s * S5_GROUP:(s + 1) * S5_GROUP, :] = shifted.astype(jnp.bfloat16)
        tw_s[g, S5_VEC:S5_VEC + S5_STATE, :] = cp1_ref[g, 0].astype(jnp.bfloat16)
        tw_s[g, S5_VEC + S5_STATE:, :] = (-cp1_ref[g, 1]).astype(jnp.bfloat16)

        v = _dot(u_ref[g], wu_ref[g])
        v_s[g] = v
        vsw_s[g] = pltpu.roll(v, S5_STATE, axis=1)

    zero = jnp.zeros((BATCH, S5_REAL), jnp.float32)
    state = [(zero, zero)] * ngroups
    for k in range(S5_NCHUNK):
        rs = slice(k * BATCH, (k + 1) * BATCH)
        for g in range(ngroups):
            x, xsw = state[g]
            xs_s[g, rs, :] = x
            ar = coef_ref[g, 0]
            ai = coef_ref[g, 1]
            state[g] = (ar * x + ai * xsw + v_s[g, rs, :], ar * xsw - ai * x + vsw_s[g, rs, :])

    for g in range(ngroups):
        y = _dot(u_ref[g], tw_s[g, 0:S5_VEC, :])
        y = y + _dot(xs_s[g].astype(jnp.bfloat16), tw_s[g, S5_VEC:, :])
        y_ref[g] = y.astype(y_ref.dtype)


def _s5_call(u_grp, layer, prm):
    gp = S5_GROUPS_PER_STEP
    weights = [prm[k] for k in ("s5_wu", "s5_bbt", "s5_cp0", "s5_cp1", "s5_dm", "s5_coef")]

    def wspec(w):
        zeros = (0,) * (w.ndim - 2)
        return pl.BlockSpec((None, gp) + w.shape[2:], lambda i: (layer, i) + zeros)

    blk = pl.BlockSpec((gp, S5_ROWS, S5_VEC), lambda i: (i, 0, 0))
    return pl.pallas_call(
        _s5_kernel,
        grid=(S5_GROUPS // gp,),
        in_specs=[blk] + [wspec(w) for w in weights],
        out_specs=blk,
        out_shape=jax.ShapeDtypeStruct((S5_GROUPS, S5_ROWS, S5_VEC), jnp.bfloat16),
        scratch_shapes=[pltpu.VMEM((gp, S5_ROWS, S5_REAL), jnp.float32)] * 3
        + [pltpu.VMEM((gp, S5_VEC + S5_REAL, S5_VEC), jnp.bfloat16)],
        compiler_params=pltpu.CompilerParams(
            dimension_semantics=("arbitrary",), vmem_limit_bytes=VMEM_LIMIT_BYTES),
        name="s5",
    )(u_grp, *weights)


def _merge_kernel(x_ref, yg_ref, ap_ref, sb_ref, wglu_ref, bglu_ref, wo_ref, o_ref, y_s):
    nchunk = MERGE_SUB // CHUNK_ROWS
    grows = MERGE_SUB // S5_CHUNK
    for sub in range(x_ref.shape[0] // MERGE_SUB):
        rs = slice(sub * MERGE_SUB, (sub + 1) * MERGE_SUB)
        gs = slice(sub * grows, (sub + 1) * grows)
        cs = slice(sub * nchunk, (sub + 1) * nchunk)
        for tile in range(D_MODEL // LANES):
            for half in range(S5_CHUNK // BLOCKS):
                w = [yg_ref[tile * BLOCKS + gi, gs, half * LANES:(half + 1) * LANES].astype(jnp.float32)
                     for gi in range(BLOCKS)]
                v = _block_transpose(w)
                for s in range(BLOCKS):
                    y_s[cs, half * BLOCKS + s, :, tile * LANES:(tile + 1) * LANES] = (
                        v[s].reshape(nchunk, BATCH, LANES))

        ys = _gelu2(y_s[cs].reshape(MERGE_SUB, D_MODEL)).astype(jnp.bfloat16)
        glu_a = _dot(ys, wglu_ref[:, 0:D_MODEL]) + bglu_ref[:, 0:D_MODEL]
        glu_b = _dot(ys, wglu_ref[:, D_MODEL:]) + bglu_ref[:, D_MODEL:]
        merged = (ap_ref[rs, :].astype(jnp.float32)
                  + sb_ref[rs, :].astype(jnp.float32) * (glu_a * _sigmoid2(glu_b)))
        o_ref[rs, :] = x_ref[rs, :] + _dot(merged.astype(jnp.bfloat16), wo_ref[...])


def _merge_call(xt, yg, ap, sb, layer, prm):
    rows = MERGE_ROWS
    weights = [prm[k] for k in ("wglu", "bglu", "wo")]
    return pl.pallas_call(
        _merge_kernel,
        grid=(ROWS // rows,),
        in_specs=[_row_spec(rows, D_MODEL), _group_spec(rows), _row_spec(rows, D_MODEL),
                  _row_spec(rows, D_MODEL)] + [_layer_spec(w, layer) for w in weights],
        out_specs=_row_spec(rows, D_MODEL),
        out_shape=jax.ShapeDtypeStruct((ROWS, D_MODEL), jnp.float32),
        scratch_shapes=[pltpu.VMEM((rows // CHUNK_ROWS, S5_CHUNK, BATCH, D_MODEL), jnp.float32)],
        compiler_params=pltpu.CompilerParams(
            dimension_semantics=("arbitrary",), vmem_limit_bytes=VMEM_LIMIT_BYTES),
        name="merge",
    )(xt, yg, ap, sb, *weights)


def _ffn_kernel(x_ref, p_ref, gffn_ref, wup_ref, cfw_ref, cfb_ref, wdn_ref,
                gple_ref, wpg_ref, wpp_ref, gfin_ref,
                o_ref, ext_s, act_s, fc_s, pt_s, *, final):
    rows = x_ref.shape[0]
    halo = (FFN_CONV - 1) * BATCH

    @pl.when(pl.program_id(0) == 0)
    def _():
        fc_s[...] = jnp.zeros_like(fc_s)

    x1 = x_ref[...]
    h2 = _rmsnorm(x1, gffn_ref[...]).astype(jnp.bfloat16)

    def conv_block(slot, col):
        up = _dot(h2, wup_ref[:, col:col + FFN_COLS])
        ext_s[slot, 0:halo, :] = fc_s[:, col:col + FFN_COLS]
        ext_s[slot, halo:halo + rows, :] = up
        fc_s[:, col:col + FFN_COLS] = up[rows - halo:rows, :]
        out = cfb_ref[:, col:col + FFN_COLS] + cfw_ref[2:3, col:col + FFN_COLS] * up
        for k in range(FFN_CONV - 1):
            out = out + cfw_ref[k:k + 1, col:col + FFN_COLS] * ext_s[slot, k * BATCH:k * BATCH + rows, :]
        return out

    for c in range(D_FF // FFN_COLS):
        col = c * FFN_COLS
        slot = 2 * (c % 2)
        act = _gelu2(conv_block(slot, col)) * conv_block(slot + 1, D_FF + col)
        act_s[:, col:col + FFN_COLS] = act.astype(jnp.bfloat16)

    x2 = x1 + _dot(act_s[...], wdn_ref[...])

    h3 = _rmsnorm(x2, gple_ref[...]).astype(jnp.bfloat16)
    gate = _sigmoid2(_dot(h3, wpg_ref[...]))
    _to_time_major(p_ref, pt_s)
    x3 = x2 + gate * _dot(pt_s[...].astype(jnp.bfloat16), wpp_ref[...])
    if final:
        x3 = _rmsnorm(x3, gfin_ref[...])
        for t in range(rows // BATCH):
            o_ref[:, t, :] = x3[t * BATCH:(t + 1) * BATCH, :]
    else:
        o_ref[...] = x3


def _ffn_call(x1, p, layer, prm, final):
    rows = FFN_ROWS
    halo = (FFN_CONV - 1) * BATCH
    weights = [prm[k] for k in ("g_ffn", "wup", "cfw", "cfb", "wdn", "g_ple", "wpg", "wpp")]
    return pl.pallas_call(
        functools.partial(_ffn_kernel, final=final),
        grid=(ROWS // rows,),
        in_specs=[_row_spec(rows, D_MODEL),
                  pl.BlockSpec((None, BATCH, rows // BATCH, PLE_DIM), lambda i: (layer, 0, i, 0))]
        + [_layer_spec(w, layer) for w in weights] + [_layer_spec(prm["g_final"], 0)],
        out_specs=_batch_major_spec(rows, D_MODEL) if final else _row_spec(rows, D_MODEL),
        out_shape=jax.ShapeDtypeStruct((BATCH, SEQ, D_MODEL) if final else (ROWS, D_MODEL),
                                       jnp.float32),
        scratch_shapes=[
            pltpu.VMEM((4, halo + rows, FFN_COLS), jnp.float32),
            pltpu.VMEM((rows, D_FF), jnp.bfloat16),
            pltpu.VMEM((halo, 2 * D_FF), jnp.float32),
            pltpu.VMEM((rows, PLE_DIM), jnp.float32),
        ],
        compiler_params=pltpu.CompilerParams(
            dimension_semantics=("arbitrary",), vmem_limit_bytes=VMEM_LIMIT_BYTES),
        name="ffn_final" if final else "ffn",
    )(x1, p, *weights, prm["g_final"])


def _s5_params(a_re, a_im, log_dt, b_re, b_im, c_re, c_im, d):
    dt = jnp.exp(log_dt)[..., None]
    lr = a_re * dt
    li = a_im * dt
    mag = jnp.exp(lr)
    lb_re = mag * jnp.cos(li)
    lb_im = mag * jnp.sin(li)
    nr = lb_re - 1.0
    ni = lb_im
    den = a_re * a_re + a_im * a_im
    coef_re = (nr * a_re + ni * a_im) / den
    coef_im = (ni * a_re - nr * a_im) / den
    bb_re = coef_re[..., None] * b_re - coef_im[..., None] * b_im
    bb_im = coef_re[..., None] * b_im + coef_im[..., None] * b_re

    n = jnp.arange(S5_CHUNK + 1, dtype=jnp.float32)
    pmag = jnp.exp(lr[..., None] * n)
    pw_re = pmag * jnp.cos(li[..., None] * n)
    pw_im = pmag * jnp.sin(li[..., None] * n)

    ct_re = jnp.swapaxes(c_re, -1, -2)[..., None, :]
    ct_im = jnp.swapaxes(c_im, -1, -2)[..., None, :]
    cp_re = ct_re * pw_re[..., None] - ct_im * pw_im[..., None]
    cp_im = ct_re * pw_im[..., None] + ct_im * pw_re[..., None]
    lead = cp_re.shape[:3]
    cp0 = jnp.stack([cp_re[..., :S5_CHUNK, :].reshape(lead + (S5_VEC,)),
                     cp_im[..., :S5_CHUNK, :].reshape(lead + (S5_VEC,))], axis=2)
    cp1 = jnp.stack([cp_re[..., 1:, :].reshape(lead + (S5_VEC,)),
                     cp_im[..., 1:, :].reshape(lead + (S5_VEC,))], axis=2)

    bbt_re = jnp.swapaxes(bb_re, -1, -2)
    bbt_im = jnp.swapaxes(bb_im, -1, -2)
    bbt = jnp.stack([bbt_re, bbt_im], axis=2)

    rev_re = jnp.swapaxes(pw_re[..., S5_CHUNK - 1::-1], -1, -2)[..., None, :]
    rev_im = jnp.swapaxes(pw_im[..., S5_CHUNK - 1::-1], -1, -2)[..., None, :]
    wu_re = rev_re * bbt_re[..., None, :, :] - rev_im * bbt_im[..., None, :, :]
    wu_im = rev_re * bbt_im[..., None, :, :] + rev_im * bbt_re[..., None, :, :]
    wu = jnp.concatenate([wu_re, wu_im], axis=-1).reshape(lead[:2] + (S5_VEC, S5_REAL))

    dm = jnp.eye(S5_GROUP, dtype=jnp.float32) * d.reshape(lead[:2] + (1, S5_GROUP))
    dm = jnp.pad(dm, ((0, 0), (0, 0), (0, 0), (0, S5_VEC - S5_GROUP)))

    l_re, l_im = pw_re[..., S5_CHUNK], pw_im[..., S5_CHUNK]
    ar = jnp.concatenate([l_re, l_re], axis=-1)
    ai = jnp.concatenate([-l_im, l_im], axis=-1)
    coef = jnp.broadcast_to(jnp.stack([ar, ai], axis=2)[..., None, :],
                            lead[:2] + (2, BATCH, S5_REAL))
    return {"s5_wu": wu.astype(jnp.bfloat16), "s5_bbt": bbt, "s5_cp0": cp0, "s5_cp1": cp1,
            "s5_dm": dm, "s5_coef": coef}


def kernel(x, p, g_mix, w_in, conv_a_w, conv_a_b, gate_x_w, gate_x_b, gate_a_w, gate_a_b, lru_lambda, w_a_out, s5_a_re, s5_a_im, s5_log_dt, s5_b_re, s5_b_im, s5_c_re, s5_c_im, s5_d, w_glu, b_glu, w_o, g_ffn, w_up, conv_f_w, conv_f_b, w_down, g_ple, w_ple_gate, w_ple_proj, g_final):
    assert x.shape == (BATCH, SEQ, D_MODEL) and p.shape == (DEPTH, BATCH, SEQ, PLE_DIM)
    bf = jnp.bfloat16
    row = lambda v: v[:, None, :]
    ones = functools.partial(jnp.full, dtype=jnp.float32)
    in_scale = jnp.concatenate([ones((O_MA,), 1.0), ones((2 * D_MODEL,), 0.5)])
    glu_scale = jnp.concatenate([ones((D_MODEL,), 0.125), ones((D_MODEL,), 0.25)])
    prm = {
        "g_mix": row(g_mix), "g_ffn": row(g_ffn), "g_ple": row(g_ple), "g_final": g_final[None, None],
        "w_in": (w_in * in_scale).astype(bf), "cw": conv_a_w, "cb": row(conv_a_b),
        "gw": (0.5 * jnp.concatenate([gate_x_w, gate_a_w], axis=-1)).astype(bf),
        "gxb": row(0.5 * gate_x_b), "gab": row(0.5 * gate_a_b), "lam": row(lru_lambda),
        "wao": (0.125 * w_a_out).astype(bf),
        "wglu": (w_glu * glu_scale).astype(bf), "bglu": row(b_glu * (2.0 * glu_scale)),
        "wo": w_o.astype(bf),
        "wup": w_up.astype(bf), "cfw": conv_f_w, "cfb": row(conv_f_b),
        "wdn": (0.5 * w_down).astype(bf), "wpg": (0.5 * w_ple_gate).astype(bf),
        "wpp": (0.5 * w_ple_proj).astype(bf),
    }
    prm.update(_s5_params(s5_a_re, s5_a_im, s5_log_dt, s5_b_re, s5_b_im, s5_c_re, s5_c_im, s5_d))

    xt = x
    for i in range(DEPTH):
        if i == 0:
            ug, ap, sb, xt = _mixer_call(xt, i, prm, batch_major_in=True)
        else:
            ug, ap, sb = _mixer_call(xt, i, prm, batch_major_in=False)
        yg = _s5_call(ug, i, prm)
        x1 = _merge_call(xt, yg, ap, sb, i, prm)
        xt = _ffn_call(x1, p, i, prm, final=(i == DEPTH - 1))
    return xt
```

```python
import functools
import math

import jax
import jax.numpy as jnp
from jax import lax
from jax.experimental import pallas as pl
from jax.experimental.pallas import tpu as pltpu

D_MODEL = 1024
BATCH = 8
SEQ = 4096
DEPTH = 2
PLE_DIM = 256
LRU_WIDTH = 1280
LRU_HEADS = 10
LRU_HEAD_DIM = 128
LRU_CONV = 4
LRU_C = 8.0
S5_GROUP = 16
S5_GROUPS = 64
S5_STATE = 64
D_FF = 3 * D_MODEL
FFN_CONV = 3
EPS = 1e-6

LANES = 128
ROWS = SEQ * BATCH
PAIR = 2 * LRU_HEAD_DIM
N_PAIRS = LRU_WIDTH // PAIR
S5_CHUNK = 16
S5_VEC = S5_CHUNK * S5_GROUP
S5_NCHUNK = SEQ // S5_CHUNK
S5_ROWS = S5_NCHUNK * BATCH
S5_REAL = 2 * S5_STATE
CHUNK_ROWS = S5_CHUNK * BATCH
BLOCKS = LANES // S5_GROUP

O_GB = LRU_WIDTH
O_UB = 2 * LRU_WIDTH
O_MA = O_UB + D_MODEL
O_MB = O_MA + D_MODEL

MIXER_ROWS = 512
MERGE_ROWS = 1024
MERGE_SUB = 256
FFN_ROWS = 512
FFN_COLS = 512
S5_GROUPS_PER_STEP = 4
VMEM_LIMIT_BYTES = 56 * 1024 * 1024

_GELU_C = math.sqrt(2.0 / math.pi)


def _gelu2(x):
    return x * (1.0 + jnp.tanh(x * (_GELU_C + (_GELU_C * 0.044715) * (x * x))))


def _sigmoid2(half_x):
    return jnp.tanh(half_x) + 1.0


def _rmsnorm(x, g):
    ms = jnp.mean(x * x, axis=-1, keepdims=True)
    return x * lax.rsqrt(ms + EPS) * g


def _dot(a, b):
    return jnp.dot(a, b, preferred_element_type=jnp.float32)


def _layer_spec(w, layer):
    zeros = (0,) * (w.ndim - 1)
    return pl.BlockSpec((None,) + w.shape[1:], lambda i: (layer,) + zeros,
                        pipeline_mode=pl.Buffered(1))


def _row_spec(rows, cols):
    return pl.BlockSpec((rows, cols), lambda i: (i, 0))


def _batch_major_spec(rows, cols):
    return pl.BlockSpec((BATCH, rows // BATCH, cols), lambda i: (0, i, 0))


def _to_time_major(src_ref, dst_ref):
    for t in range(src_ref.shape[1]):
        dst_ref[t * BATCH:(t + 1) * BATCH, :] = src_ref[:, t, :]


def _group_spec(rows):
    return pl.BlockSpec((S5_GROUPS, rows // S5_CHUNK, S5_VEC), lambda i: (0, i, 0))


def _lane_roll(a, shift):
    if a.dtype == jnp.float32:
        return pltpu.roll(a, shift, axis=1)
    return jnp.concatenate([a[:, LANES - shift:], a[:, :LANES - shift]], axis=1)


def _block_transpose(v):
    blk = lax.broadcasted_iota(jnp.int32, v[0].shape, 1) // S5_GROUP
    for d in (4, 2, 1):
        bit = (blk & d) != 0
        out = list(v)
        for i in range(BLOCKS):
            if i & d == 0:
                lo, hi = v[i], v[i + d]
                out[i] = jnp.where(bit, _lane_roll(hi, S5_GROUP * d), lo)
                out[i + d] = jnp.where(bit, hi, _lane_roll(lo, LANES - S5_GROUP * d))
        v = out
    return v


def _mixer_kernel(x_ref, g_ref, win_ref, cw_ref, cb_ref, gw_ref, gxb_ref, gab_ref, lam_ref,
                  wao_ref, ug_ref, ap_ref, sb_ref, *rest, batch_major_in):
    if batch_major_in:
        xt_ref, ext_s, a_s, u_s, gb_s, ma_s, ub_s, hg_s, hc_s = rest
        _to_time_major(x_ref, xt_ref)
        x_ref = xt_ref
    else:
        ext_s, a_s, u_s, gb_s, ma_s, ub_s, hg_s, hc_s = rest
    rows = x_ref.shape[0]
    nchunk = rows // CHUNK_ROWS
    halo = (LRU_CONV - 1) * BATCH

    @pl.when(pl.program_id(0) == 0)
    def _():
        ext_s[0:halo, :] = jnp.zeros((halo, LRU_WIDTH), jnp.float32)
        hc_s[...] = jnp.zeros_like(hc_s)

    hb = _rmsnorm(x_ref[...], g_ref[...]).astype(jnp.bfloat16)

    lam = lam_ref[...]
    log_coef = (0.5 * LRU_C) * (jnp.minimum(lam, 0.0) - jnp.log1p(jnp.exp(-jnp.abs(lam))))

    for j in range(N_PAIRS):
        c0 = j * PAIR
        xa = _dot(hb, win_ref[:, c0:c0 + PAIR])
        ext_s[halo:halo + rows, c0:c0 + PAIR] = xa
        xc = cb_ref[:, c0:c0 + PAIR] + cw_ref[3:4, c0:c0 + PAIR] * xa
        for k in range(LRU_CONV - 1):
            xc = xc + cw_ref[k:k + 1, c0:c0 + PAIR] * ext_s[k * BATCH:k * BATCH + rows, c0:c0 + PAIR]
        xcb = xc.astype(jnp.bfloat16)
        for hh in range(2):
            lo = hh * LRU_HEAD_DIM
            cs = slice(c0 + lo, c0 + lo + LRU_HEAD_DIM)
            gates = _dot(xcb[:, lo:lo + LRU_HEAD_DIM], gw_ref[2 * j + hh])
            gx2 = _sigmoid2(gates[:, :LRU_HEAD_DIM] + gxb_ref[:, cs])
            log_a = log_coef[:, cs] * jnp.tanh(gates[:, LRU_HEAD_DIM:] + gab_ref[:, cs]) + log_coef[:, cs]
            a = jnp.exp(log_a)
            mult = jnp.sqrt(jnp.maximum((1.0 - a) * (1.0 + a), 1e-12))
            a_s[:, cs] = a
            u_s[:, cs] = mult * (gx2 * xc[:, lo:lo + LRU_HEAD_DIM])
        gb_s[:, c0:c0 + PAIR] = _gelu2(_dot(hb, win_ref[:, O_GB + c0:O_GB + c0 + PAIR]))

    ext_s[0:halo, :] = ext_s[rows:rows + halo, :]

    ma_s[...] = _sigmoid2(_dot(hb, win_ref[:, O_MA:O_MA + D_MODEL]))
    sb_ref[...] = _sigmoid2(_dot(hb, win_ref[:, O_MB:O_MB + D_MODEL])).astype(sb_ref.dtype)

    ub_s[...] = _dot(hb, win_ref[:, O_UB:O_UB + D_MODEL]).reshape(nchunk, S5_CHUNK, BATCH, D_MODEL)
    for tile in range(D_MODEL // LANES):
        for half in range(S5_CHUNK // BLOCKS):
            v = [ub_s[:, half * BLOCKS + s, :, tile * LANES:(tile + 1) * LANES].reshape(nchunk * BATCH, LANES)
                 for s in range(BLOCKS)]
            w = _block_transpose(v)
            for gi in range(BLOCKS):
                ug_ref[tile * BLOCKS + gi, :, half * LANES:(half + 1) * LANES] = w[gi].astype(ug_ref.dtype)

    h = hc_s[...]
    for t in range(rows // BATCH):
        r = t * BATCH
        h = a_s[r:r + BATCH, :] * h + u_s[r:r + BATCH, :]
        hg_s[r:r + BATCH, :] = h * gb_s[r:r + BATCH, :]
    hc_s[...] = h

    ya = _dot(hg_s[...].astype(jnp.bfloat16), wao_ref[...])
    ap_ref[...] = (ma_s[...] * ya).astype(ap_ref.dtype)


def _mixer_call(x, layer, prm, batch_major_in):
    rows = MIXER_ROWS
    halo = (LRU_CONV - 1) * BATCH
    out = jax.ShapeDtypeStruct((ROWS, D_MODEL), jnp.bfloat16)
    grp = jax.ShapeDtypeStruct((S5_GROUPS, S5_ROWS, S5_VEC), jnp.bfloat16)
    weights = [prm[k] for k in ("g_mix", "w_in", "cw", "cb", "gw", "gxb", "gab", "lam", "wao")]
    f32 = jnp.float32
    x_spec = _batch_major_spec(rows, D_MODEL) if batch_major_in else _row_spec(rows, D_MODEL)
    extra = batch_major_in * [jax.ShapeDtypeStruct((ROWS, D_MODEL), f32)]
    return pl.pallas_call(
        functools.partial(_mixer_kernel, batch_major_in=batch_major_in),
        grid=(ROWS // rows,),
        in_specs=[x_spec] + [_layer_spec(w, layer) for w in weights],
        out_specs=[_group_spec(rows)] + [_row_spec(rows, D_MODEL)] * (2 + len(extra)),
        out_shape=[grp, out, out] + extra,
        scratch_shapes=[
            pltpu.VMEM((halo + rows, LRU_WIDTH), f32),
            pltpu.VMEM((rows, LRU_WIDTH), f32),
            pltpu.VMEM((rows, LRU_WIDTH), f32),
            pltpu.VMEM((rows, LRU_WIDTH), f32),
            pltpu.VMEM((rows, D_MODEL), f32),
            pltpu.VMEM((rows // CHUNK_ROWS, S5_CHUNK, BATCH, D_MODEL), f32),
            pltpu.VMEM((rows, LRU_WIDTH), f32),
            pltpu.VMEM((BATCH, LRU_WIDTH), f32),
        ],
        compiler_params=pltpu.CompilerParams(
            dimension_semantics=("arbitrary",), vmem_limit_bytes=VMEM_LIMIT_BYTES),
        name="mixer_in" if batch_major_in else "mixer",
    )(x, *weights)


def _s5_kernel(u_ref, wu_ref, bbt_ref, cp0_ref, cp1_ref, dm_ref, coef_ref, y_ref,
               v_s, vsw_s, xs_s, tw_s):
    ngroups = u_ref.shape[0]
    hi = lax.Precision.HIGHEST
    lane = lax.broadcasted_iota(jnp.int32, (S5_GROUP, S5_VEC), 1)
    for g in range(ngroups):
        taps = (jnp.dot(bbt_ref[g, 0], cp0_ref[g, 0], precision=hi, preferred_element_type=jnp.float32)
                - jnp.dot(bbt_ref[g, 1], cp0_ref[g, 1], precision=hi, preferred_element_type=jnp.float32)
                + dm_ref[g])
        tw_s[g, 0:S5_GROUP, :] = taps.astype(jnp.bfloat16)
        for s in range(1, S5_CHUNK):
            shifted = jnp.where(lane >= s * S5_GROUP, pltpu.roll(taps, s * S5_GROUP, axis=1), 0.0)
            tw_s[g, s * S5_GROUP:(s + 1) * S5_GROUP, :] = shifted.astype(jnp.bfloat16)
        tw_s[g, S5_VEC:S5_VEC + S5_STATE, :] = cp1_ref[g, 0].astype(jnp.bfloat16)
        tw_s[g, S5_VEC + S5_STATE:, :] = (-cp1_ref[g, 1]).astype(jnp.bfloat16)

        v = _dot(u_ref[g], wu_ref[g])
        v_s[g] = v
        vsw_s[g] = pltpu.roll(v, S5_STATE, axis=1)

    zero = jnp.zeros((BATCH, S5_REAL), jnp.float32)
    state = [(zero, zero)] * ngroups
    for k in range(S5_NCHUNK):
        rs = slice(k * BATCH, (k + 1) * BATCH)
        for g in range(ngroups):
            x, xsw = state[g]
            xs_s[g, rs, :] = x
            ar = coef_ref[g, 0]
            ai = coef_ref[g, 1]
            state[g] = (ar * x + ai * xsw + v_s[g, rs, :], ar * xsw - ai * x + vsw_s[g, rs, :])

    for g in range(ngroups):
        y = _dot(u_ref[g], tw_s[g, 0:S5_VEC, :])
        y = y + _dot(xs_s[g].astype(jnp.bfloat16), tw_s[g, S5_VEC:, :])
        y_ref[g] = y.astype(y_ref.dtype)


def _s5_call(u_grp, layer, prm):
    gp = S5_GROUPS_PER_STEP
    weights = [prm[k] for k in ("s5_wu", "s5_bbt", "s5_cp0", "s5_cp1", "s5_dm", "s5_coef")]

    def wspec(w):
        zeros = (0,) * (w.ndim - 2)
        return pl.BlockSpec((None, gp) + w.shape[2:], lambda i: (layer, i) + zeros)

    blk = pl.BlockSpec((gp, S5_ROWS, S5_VEC), lambda i: (i, 0, 0))
    return pl.pallas_call(
        _s5_kernel,
        grid=(S5_GROUPS // gp,),
        in_specs=[blk] + [wspec(w) for w in weights],
        out_specs=blk,
        out_shape=jax.ShapeDtypeStruct((S5_GROUPS, S5_ROWS, S5_VEC), jnp.bfloat16),
        scratch_shapes=[pltpu.VMEM((gp, S5_ROWS, S5_REAL), jnp.float32)] * 3
        + [pltpu.VMEM((gp, S5_VEC + S5_REAL, S5_VEC), jnp.bfloat16)],
        compiler_params=pltpu.CompilerParams(
            dimension_semantics=("arbitrary",), vmem_limit_bytes=VMEM_LIMIT_BYTES),
        name="s5",
    )(u_grp, *weights)


def _merge_kernel(x_ref, yg_ref, ap_ref, sb_ref, wglu_ref, bglu_ref, wo_ref, o_ref, ys_s):
    nchunk = MERGE_SUB // CHUNK_ROWS
    assert nchunk * BATCH == 16
    grows = MERGE_SUB // S5_CHUNK

    def step_major(a):
        a = a.reshape(nchunk, S5_CHUNK, BATCH, D_MODEL)
        return jnp.swapaxes(a, 0, 1).reshape(MERGE_SUB, D_MODEL)

    def chunk_major(a):
        a = a.reshape(S5_CHUNK, nchunk, BATCH, D_MODEL)
        return jnp.swapaxes(a, 0, 1).reshape(MERGE_SUB, D_MODEL)

    for sub in range(x_ref.shape[0] // MERGE_SUB):
        rs = slice(sub * MERGE_SUB, (sub + 1) * MERGE_SUB)
        gs = slice(sub * grows, (sub + 1) * grows)
        for tile in range(D_MODEL // LANES):
            ls = slice(tile * LANES, (tile + 1) * LANES)
            for half in range(S5_CHUNK // BLOCKS):
                w = [yg_ref[tile * BLOCKS + gi, gs, half * LANES:(half + 1) * LANES] for gi in range(BLOCKS)]
                v = _block_transpose(w)
                for s in range(BLOCKS):
                    step = half * BLOCKS + s
                    ys_s[sub, step * nchunk * BATCH:(step + 1) * nchunk * BATCH, ls] = (
                        _gelu2(v[s].astype(jnp.float32)).astype(jnp.bfloat16))

        ys = ys_s[sub]
        glu_a = _dot(ys, wglu_ref[:, 0:D_MODEL]) + bglu_ref[:, 0:D_MODEL]
        glu_b = _dot(ys, wglu_ref[:, D_MODEL:]) + bglu_ref[:, D_MODEL:]
        merged = (step_major(ap_ref[rs, :].astype(jnp.float32))
                  + step_major(sb_ref[rs, :].astype(jnp.float32)) * (glu_a * _sigmoid2(glu_b)))
        o_ref[rs, :] = x_ref[rs, :] + chunk_major(_dot(merged.astype(jnp.bfloat16), wo_ref[...]))


def _merge_call(xt, yg, ap, sb, layer, prm):
    rows = MERGE_ROWS
    weights = [prm[k] for k in ("wglu", "bglu", "wo")]
    return pl.pallas_call(
        _merge_kernel,
        grid=(ROWS // rows,),
        in_specs=[_row_spec(rows, D_MODEL), _group_spec(rows), _row_spec(rows, D_MODEL),
                  _row_spec(rows, D_MODEL)] + [_layer_spec(w, layer) for w in weights],
        out_specs=_row_spec(rows, D_MODEL),
        out_shape=jax.ShapeDtypeStruct((ROWS, D_MODEL), jnp.float32),
        scratch_shapes=[pltpu.VMEM((rows // MERGE_SUB, MERGE_SUB, D_MODEL), jnp.bfloat16)],
        compiler_params=pltpu.CompilerParams(
            dimension_semantics=("arbitrary",), vmem_limit_bytes=VMEM_LIMIT_BYTES),
        name="merge",
    )(xt, yg, ap, sb, *weights)


def _ffn_kernel(x_ref, p_ref, gffn_ref, wup_ref, cfw_ref, cfb_ref, wdn_ref,
                gple_ref, wpg_ref, wpp_ref, gfin_ref,
                o_ref, act_s, fc_s, pt_s, *, final):
    rows = x_ref.shape[0]
    halo = (FFN_CONV - 1) * BATCH

    @pl.when(pl.program_id(0) == 0)
    def _():
        fc_s[...] = jnp.zeros_like(fc_s)

    x1 = x_ref[...]
    h2 = _rmsnorm(x1, gffn_ref[...]).astype(jnp.bfloat16)

    def conv_block(col):
        up = _dot(h2, wup_ref[:, col:col + FFN_COLS])
        ext = jnp.concatenate([fc_s[:, col:col + FFN_COLS], up], axis=0)
        fc_s[:, col:col + FFN_COLS] = up[rows - halo:rows, :]
        out = cfb_ref[:, col:col + FFN_COLS] + cfw_ref[2:3, col:col + FFN_COLS] * up
        for k in range(FFN_CONV - 1):
            out = out + cfw_ref[k:k + 1, col:col + FFN_COLS] * ext[k * BATCH:k * BATCH + rows, :]
        return out

    for c in range(D_FF // FFN_COLS):
        col = c * FFN_COLS
        act = _gelu2(conv_block(col)) * conv_block(D_FF + col)
        act_s[:, col:col + FFN_COLS] = act.astype(jnp.bfloat16)

    x2 = x1 + _dot(act_s[...], wdn_ref[...])

    h3 = _rmsnorm(x2, gple_ref[...]).astype(jnp.bfloat16)
    gate = _sigmoid2(_dot(h3, wpg_ref[...]))
    _to_time_major(p_ref, pt_s)
    x3 = x2 + gate * _dot(pt_s[...].astype(jnp.bfloat16), wpp_ref[...])
    if final:
        x3 = _rmsnorm(x3, gfin_ref[...])
        for t in range(rows // BATCH):
            o_ref[:, t, :] = x3[t * BATCH:(t + 1) * BATCH, :]
    else:
        o_ref[...] = x3


def _ffn_call(x1, p, layer, prm, final):
    rows = FFN_ROWS
    halo = (FFN_CONV - 1) * BATCH
    weights = [prm[k] for k in ("g_ffn", "wup", "cfw", "cfb", "wdn", "g_ple", "wpg", "wpp")]
    return pl.pallas_call(
        functools.partial(_ffn_kernel, final=final),
        grid=(ROWS // rows,),
        in_specs=[_row_spec(rows, D_MODEL),
                  pl.BlockSpec((None, BATCH, rows // BATCH, PLE_DIM), lambda i: (layer, 0, i, 0))]
        + [_layer_spec(w, layer) for w in weights] + [_layer_spec(prm["g_final"], 0)],
        out_specs=_batch_major_spec(rows, D_MODEL) if final else _row_spec(rows, D_MODEL),
        out_shape=jax.ShapeDtypeStruct((BATCH, SEQ, D_MODEL) if final else (ROWS, D_MODEL),
                                       jnp.float32),
        scratch_shapes=[
            pltpu.VMEM((rows, D_FF), jnp.bfloat16),
            pltpu.VMEM((halo, 2 * D_FF), jnp.float32),
            pltpu.VMEM((rows, PLE_DIM), jnp.float32),
        ],
        compiler_params=pltpu.CompilerParams(
            dimension_semantics=("arbitrary",), vmem_limit_bytes=VMEM_LIMIT_BYTES),
        name="ffn_final" if final else "ffn",
    )(x1, p, *weights, prm["g_final"])


def _s5_params(a_re, a_im, log_dt, b_re, b_im, c_re, c_im, d):
    dt = jnp.exp(log_dt)[..., None]
    lr = a_re * dt
    li = a_im * dt
    mag = jnp.exp(lr)
    lb_re = mag * jnp.cos(li)
    lb_im = mag * jnp.sin(li)
    nr = lb_re - 1.0
    ni = lb_im
    den = a_re * a_re + a_im * a_im
    coef_re = (nr * a_re + ni * a_im) / den
    coef_im = (ni * a_re - nr * a_im) / den
    bb_re = coef_re[..., None] * b_re - coef_im[..., None] * b_im
    bb_im = coef_re[..., None] * b_im + coef_im[..., None] * b_re

    n = jnp.arange(S5_CHUNK + 1, dtype=jnp.float32)
    pmag = jnp.exp(lr[..., None] * n)
    pw_re = pmag * jnp.cos(li[..., None] * n)
    pw_im = pmag * jnp.sin(li[..., None] * n)

    ct_re = jnp.swapaxes(c_re, -1, -2)[..., None, :]
    ct_im = jnp.swapaxes(c_im, -1, -2)[..., None, :]
    cp_re = ct_re * pw_re[..., None] - ct_im * pw_im[..., None]
    cp_im = ct_re * pw_im[..., None] + ct_im * pw_re[..., None]
    lead = cp_re.shape[:3]
    cp0 = jnp.stack([cp_re[..., :S5_CHUNK, :].reshape(lead + (S5_VEC,)),
                     cp_im[..., :S5_CHUNK, :].reshape(lead + (S5_VEC,))], axis=2)
    cp1 = jnp.stack([cp_re[..., 1:, :].reshape(lead + (S5_VEC,)),
                     cp_im[..., 1:, :].reshape(lead + (S5_VEC,))], axis=2)

    bbt_re = jnp.swapaxes(bb_re, -1, -2)
    bbt_im = jnp.swapaxes(bb_im, -1, -2)
    bbt = jnp.stack([bbt_re, bbt_im], axis=2)

    rev_re = jnp.swapaxes(pw_re[..., S5_CHUNK - 1::-1], -1, -2)[..., None, :]
    rev_im = jnp.swapaxes(pw_im[..., S5_CHUNK - 1::-1], -1, -2)[..., None, :]
    wu_re = rev_re * bbt_re[..., None, :, :] - rev_im * bbt_im[..., None, :, :]
    wu_im = rev_re * bbt_im[..., None, :, :] + rev_im * bbt_re[..., None, :, :]
    wu = jnp.concatenate([wu_re, wu_im], axis=-1).reshape(lead[:2] + (S5_VEC, S5_REAL))

    dm = jnp.eye(S5_GROUP, dtype=jnp.float32) * d.reshape(lead[:2] + (1, S5_GROUP))
    dm = jnp.pad(dm, ((0, 0), (0, 0), (0, 0), (0, S5_VEC - S5_GROUP)))

    l_re, l_im = pw_re[..., S5_CHUNK], pw_im[..., S5_CHUNK]
    ar = jnp.concatenate([l_re, l_re], axis=-1)
    ai = jnp.concatenate([-l_im, l_im], axis=-1)
    coef = jnp.broadcast_to(jnp.stack([ar, ai], axis=2)[..., None, :],
                            lead[:2] + (2, BATCH, S5_REAL))
    return {"s5_wu": wu.astype(jnp.bfloat16), "s5_bbt": bbt, "s5_cp0": cp0, "s5_cp1": cp1,
            "s5_dm": dm, "s5_coef": coef}


def kernel(x, p, g_mix, w_in, conv_a_w, conv_a_b, gate_x_w, gate_x_b, gate_a_w, gate_a_b, lru_lambda, w_a_out, s5_a_re, s5_a_im, s5_log_dt, s5_b_re, s5_b_im, s5_c_re, s5_c_im, s5_d, w_glu, b_glu, w_o, g_ffn, w_up, conv_f_w, conv_f_b, w_down, g_ple, w_ple_gate, w_ple_proj, g_final):
    assert x.shape == (BATCH, SEQ, D_MODEL) and p.shape == (DEPTH, BATCH, SEQ, PLE_DIM)
    bf = jnp.bfloat16
    row = lambda v: v[:, None, :]
    ones = functools.partial(jnp.full, dtype=jnp.float32)
    in_scale = jnp.concatenate([ones((O_MA,), 1.0), ones((2 * D_MODEL,), 0.5)])
    glu_scale = jnp.concatenate([ones((D_MODEL,), 0.125), ones((D_MODEL,), 0.25)])
    prm = {
        "g_mix": row(g_mix), "g_ffn": row(g_ffn), "g_ple": row(g_ple), "g_final": g_final[None, None],
        "w_in": (w_in * in_scale).astype(bf), "cw": conv_a_w, "cb": row(conv_a_b),
        "gw": (0.5 * jnp.concatenate([gate_x_w, gate_a_w], axis=-1)).astype(bf),
        "gxb": row(0.5 * gate_x_b), "gab": row(0.5 * gate_a_b), "lam": row(lru_lambda),
        "wao": (0.125 * w_a_out).astype(bf),
        "wglu": (w_glu * glu_scale).astype(bf), "bglu": row(b_glu * (2.0 * glu_scale)),
        "wo": w_o.astype(bf),
        "wup": w_up.astype(bf), "cfw": conv_f_w, "cfb": row(conv_f_b),
        "wdn": (0.5 * w_down).astype(bf), "wpg": (0.5 * w_ple_gate).astype(bf),
        "wpp": (0.5 * w_ple_proj).astype(bf),
    }
    prm.update(_s5_params(s5_a_re, s5_a_im, s5_log_dt, s5_b_re, s5_b_im, s5_c_re, s5_c_im, s5_d))

    xt = x
    for i in range(DEPTH):
        if i == 0:
            ug, ap, sb, xt = _mixer_call(xt, i, prm, batch_major_in=True)
        else:
            ug, ap, sb = _mixer_call(xt, i, prm, batch_major_in=False)
        yg = _s5_call(ug, i, prm)
        x1 = _merge_call(xt, yg, ap, sb, i, prm)
        xt = _ffn_call(x1, p, i, prm, final=(i == DEPTH - 1))
    return xt
```

```python
import functools
import math

import jax
import jax.numpy as jnp
from jax import lax
from jax.experimental import pallas as pl
from jax.experimental.pallas import tpu as pltpu

D_MODEL = 1024
BATCH = 8
SEQ = 4096
DEPTH = 2
PLE_DIM = 256
LRU_WIDTH = 1280
LRU_HEADS = 10
LRU_HEAD_DIM = 128
LRU_CONV = 4
LRU_C = 8.0
S5_GROUP = 16
S5_GROUPS = 64
S5_STATE = 64
D_FF = 3 * D_MODEL
FFN_CONV = 3
EPS = 1e-6

LANES = 128
ROWS = SEQ * BATCH
PAIR = 2 * LRU_HEAD_DIM
N_PAIRS = LRU_WIDTH // PAIR
S5_CHUNK = 16
S5_VEC = S5_CHUNK * S5_GROUP
S5_NCHUNK = SEQ // S5_CHUNK
S5_ROWS = S5_NCHUNK * BATCH
S5_REAL = 2 * S5_STATE
CHUNK_ROWS = S5_CHUNK * BATCH
BLOCKS = LANES // S5_GROUP

O_GB = LRU_WIDTH
O_UB = 2 * LRU_WIDTH
O_MA = O_UB + D_MODEL
O_MB = O_MA + D_MODEL

MIXER_ROWS = 512
MERGE_ROWS = 2048
MERGE_SUB = 256
FFN_ROWS = 512
FFN_COLS = 512
S5_GROUPS_PER_STEP = 4
VMEM_LIMIT_BYTES = 56 * 1024 * 1024

_GELU_C = math.sqrt(2.0 / math.pi)


def _gelu2(x):
    return x * (1.0 + jnp.tanh(x * (_GELU_C + (_GELU_C * 0.044715) * (x * x))))


def _sigmoid2(half_x):
    return jnp.tanh(half_x) + 1.0


def _rmsnorm(x, g):
    ms = jnp.mean(x * x, axis=-1, keepdims=True)
    return x * lax.rsqrt(ms + EPS) * g


def _dot(a, b):
    return jnp.dot(a, b, preferred_element_type=jnp.float32)


def _layer_spec(w, layer):
    zeros = (0,) * (w.ndim - 1)
    return pl.BlockSpec((None,) + w.shape[1:], lambda i: (layer,) + zeros,
                        pipeline_mode=pl.Buffered(1))


def _row_spec(rows, cols):
    return pl.BlockSpec((rows, cols), lambda i: (i, 0))


def _batch_major_spec(rows, cols):
    return pl.BlockSpec((BATCH, rows // BATCH, cols), lambda i: (0, i, 0))


def _to_time_major(src_ref, dst_ref):
    for t in range(src_ref.shape[1]):
        dst_ref[t * BATCH:(t + 1) * BATCH, :] = src_ref[:, t, :]


def _from_time_major(src, dst_ref):
    for t in range(dst_ref.shape[1]):
        dst_ref[:, t, :] = src[t * BATCH:(t + 1) * BATCH, :]


def _group_spec(rows):
    return pl.BlockSpec((S5_GROUPS, rows // S5_CHUNK, S5_VEC), lambda i: (0, i, 0))


def _lane_roll(a, shift):
    if a.dtype == jnp.float32:
        return pltpu.roll(a, shift, axis=1)
    return jnp.concatenate([a[:, LANES - shift:], a[:, :LANES - shift]], axis=1)


def _block_transpose(v):
    blk = lax.broadcasted_iota(jnp.int32, v[0].shape, 1) // S5_GROUP
    for d in (4, 2, 1):
        bit = (blk & d) != 0
        out = list(v)
        for i in range(BLOCKS):
            if i & d == 0:
                lo, hi = v[i], v[i + d]
                out[i] = jnp.where(bit, _lane_roll(hi, S5_GROUP * d), lo)
                out[i + d] = jnp.where(bit, hi, _lane_roll(lo, LANES - S5_GROUP * d))
        v = out
    return v


def _mixer_kernel(x_ref, g_ref, win_ref, cw_ref, cb_ref, gw_ref, gxb_ref, gab_ref, lam_ref,
                  wao_ref, ug_ref, ap_ref, sb_ref, *rest, batch_major_in):
    if batch_major_in:
        xt_ref, ext_s, a_s, u_s, gb_s, ma_s, ub_s, hg_s, hc_s = rest
        _to_time_major(x_ref, xt_ref)
        x_ref = xt_ref
    else:
        ext_s, a_s, u_s, gb_s, ma_s, ub_s, hg_s, hc_s = rest
    rows = x_ref.shape[0]
    nchunk = rows // CHUNK_ROWS
    halo = (LRU_CONV - 1) * BATCH

    @pl.when(pl.program_id(0) == 0)
    def _():
        ext_s[0:halo, :] = jnp.zeros((halo, LRU_WIDTH), jnp.float32)
        hc_s[...] = jnp.zeros_like(hc_s)

    hb = _rmsnorm(x_ref[...], g_ref[...]).astype(jnp.bfloat16)

    lam = lam_ref[...]
    log_coef = (0.5 * LRU_C) * (jnp.minimum(lam, 0.0) - jnp.log1p(jnp.exp(-jnp.abs(lam))))

    for j in range(N_PAIRS):
        c0 = j * PAIR
        xa = _dot(hb, win_ref[:, c0:c0 + PAIR])
        ext_s[halo:halo + rows, c0:c0 + PAIR] = xa
        xc = cb_ref[:, c0:c0 + PAIR] + cw_ref[3:4, c0:c0 + PAIR] * xa
        for k in range(LRU_CONV - 1):
            xc = xc + cw_ref[k:k + 1, c0:c0 + PAIR] * ext_s[k * BATCH:k * BATCH + rows, c0:c0 + PAIR]
        xcb = xc.astype(jnp.bfloat16)
        for hh in range(2):
            lo = hh * LRU_HEAD_DIM
            cs = slice(c0 + lo, c0 + lo + LRU_HEAD_DIM)
            gates = _dot(xcb[:, lo:lo + LRU_HEAD_DIM], gw_ref[2 * j + hh])
            gx2 = _sigmoid2(gates[:, :LRU_HEAD_DIM] + gxb_ref[:, cs])
            log_a = log_coef[:, cs] * jnp.tanh(gates[:, LRU_HEAD_DIM:] + gab_ref[:, cs]) + log_coef[:, cs]
            a = jnp.exp(log_a)
            mult = jnp.sqrt(jnp.maximum((1.0 - a) * (1.0 + a), 1e-12))
            a_s[:, cs] = a
            u_s[:, cs] = mult * (gx2 * xc[:, lo:lo + LRU_HEAD_DIM])
        gb_s[:, c0:c0 + PAIR] = _gelu2(_dot(hb, win_ref[:, O_GB + c0:O_GB + c0 + PAIR]))

    ext_s[0:halo, :] = ext_s[rows:rows + halo, :]

    ma_s[...] = _sigmoid2(_dot(hb, win_ref[:, O_MA:O_MA + D_MODEL]))
    sb_ref[...] = _sigmoid2(_dot(hb, win_ref[:, O_MB:O_MB + D_MODEL])).astype(sb_ref.dtype)

    ub_s[...] = _dot(hb, win_ref[:, O_UB:O_UB + D_MODEL]).reshape(nchunk, S5_CHUNK, BATCH, D_MODEL)
    for tile in range(D_MODEL // LANES):
        for half in range(S5_CHUNK // BLOCKS):
            v = [ub_s[:, half * BLOCKS + s, :, tile * LANES:(tile + 1) * LANES].reshape(nchunk * BATCH, LANES)
                 for s in range(BLOCKS)]
            w = _block_transpose(v)
            for gi in range(BLOCKS):
                ug_ref[tile * BLOCKS + gi, :, half * LANES:(half + 1) * LANES] = w[gi].astype(ug_ref.dtype)

    h = hc_s[...]
    for t in range(rows // BATCH):
        r = t * BATCH
        h = a_s[r:r + BATCH, :] * h + u_s[r:r + BATCH, :]
        hg_s[r:r + BATCH, :] = h * gb_s[r:r + BATCH, :]
    hc_s[...] = h

    ya = _dot(hg_s[...].astype(jnp.bfloat16), wao_ref[...])
    ap_ref[...] = (ma_s[...] * ya).astype(ap_ref.dtype)


def _mixer_call(x, layer, prm, batch_major_in):
    rows = MIXER_ROWS
    halo = (LRU_CONV - 1) * BATCH
    out = jax.ShapeDtypeStruct((ROWS, D_MODEL), jnp.bfloat16)
    grp = jax.ShapeDtypeStruct((S5_GROUPS, S5_ROWS, S5_VEC), jnp.bfloat16)
    weights = [prm[k] for k in ("g_mix", "w_in", "cw", "cb", "gw", "gxb", "gab", "lam", "wao")]
    f32 = jnp.float32
    x_spec = _batch_major_spec(rows, D_MODEL) if batch_major_in else _row_spec(rows, D_MODEL)
    extra = batch_major_in * [jax.ShapeDtypeStruct((ROWS, D_MODEL), f32)]
    return pl.pallas_call(
        functools.partial(_mixer_kernel, batch_major_in=batch_major_in),
        grid=(ROWS // rows,),
        in_specs=[x_spec] + [_layer_spec(w, layer) for w in weights],
        out_specs=[_group_spec(rows)] + [_row_spec(rows, D_MODEL)] * (2 + len(extra)),
        out_shape=[grp, out, out] + extra,
        scratch_shapes=[
            pltpu.VMEM((halo + rows, LRU_WIDTH), f32),
            pltpu.VMEM((rows, LRU_WIDTH), f32),
            pltpu.VMEM((rows, LRU_WIDTH), f32),
            pltpu.VMEM((rows, LRU_WIDTH), f32),
            pltpu.VMEM((rows, D_MODEL), f32),
            pltpu.VMEM((rows // CHUNK_ROWS, S5_CHUNK, BATCH, D_MODEL), f32),
            pltpu.VMEM((rows, LRU_WIDTH), f32),
            pltpu.VMEM((BATCH, LRU_WIDTH), f32),
        ],
        compiler_params=pltpu.CompilerParams(
            dimension_semantics=("arbitrary",), vmem_limit_bytes=VMEM_LIMIT_BYTES),
        name="mixer_in" if batch_major_in else "mixer",
    )(x, *weights)


def _s5_kernel(u_ref, wu_ref, bbt_ref, cp0_ref, cp1_ref, dm_ref, coef_ref, y_ref,
               v_s, vsw_s, xs_s, tw_s):
    ngroups = u_ref.shape[0]
    hi = lax.Precision.HIGHEST
    lane = lax.broadcasted_iota(jnp.int32, (S5_GROUP, S5_VEC), 1)
    for g in range(ngroups):
        taps = (jnp.dot(bbt_ref[g, 0], cp0_ref[g, 0], precision=hi, preferred_element_type=jnp.float32)
                - jnp.dot(bbt_ref[g, 1], cp0_ref[g, 1], precision=hi, preferred_element_type=jnp.float32)
                + dm_ref[g])
        tw_s[g, 0:S5_GROUP, :] = taps.astype(jnp.bfloat16)
        for s in range(1, S5_CHUNK):
            shifted = jnp.where(lane >= s * S5_GROUP, pltpu.roll(taps, s * S5_GROUP, axis=1), 0.0)
            tw_s[g, s * S5_GROUP:(s + 1) * S5_GROUP, :] = shifted.astype(jnp.bfloat16)
        tw_s[g, S5_VEC:S5_VEC + S5_STATE, :] = cp1_ref[g, 0].astype(jnp.bfloat16)
        tw_s[g, S5_VEC + S5_STATE:, :] = (-cp1_ref[g, 1]).astype(jnp.bfloat16)

        v = _dot(u_ref[g], wu_ref[g])
        v_s[g] = v
        vsw_s[g] = pltpu.roll(v, S5_STATE, axis=1)

    zero = jnp.zeros((BATCH, S5_REAL), jnp.float32)
    state = [(zero, zero)] * ngroups
    for k in range(S5_NCHUNK):
        rs = slice(k * BATCH, (k + 1) * BATCH)
        for g in range(ngroups):
            x, xsw = state[g]
            xs_s[g, rs, :] = x
            ar = coef_ref[g, 0]
            ai = coef_ref[g, 1]
            state[g] = (ar * x + ai * xsw + v_s[g, rs, :], ar * xsw - ai * x + vsw_s[g, rs, :])

    for g in range(ngroups):
        y = _dot(u_ref[g], tw_s[g, 0:S5_VEC, :])
        y = y + _dot(xs_s[g].astype(jnp.bfloat16), tw_s[g, S5_VEC:, :])
        y_ref[g] = y.astype(y_ref.dtype)


def _s5_call(u_grp, layer, prm):
    gp = S5_GROUPS_PER_STEP
    weights = [prm[k] for k in ("s5_wu", "s5_bbt", "s5_cp0", "s5_cp1", "s5_dm", "s5_coef")]

    def wspec(w):
        zeros = (0,) * (w.ndim - 2)
        return pl.BlockSpec((None, gp) + w.shape[2:], lambda i: (layer, i) + zeros)

    blk = pl.BlockSpec((gp, S5_ROWS, S5_VEC), lambda i: (i, 0, 0))
    return pl.pallas_call(
        _s5_kernel,
        grid=(S5_GROUPS // gp,),
        in_specs=[blk] + [wspec(w) for w in weights],
        out_specs=blk,
        out_shape=jax.ShapeDtypeStruct((S5_GROUPS, S5_ROWS, S5_VEC), jnp.bfloat16),
        scratch_shapes=[pltpu.VMEM((gp, S5_ROWS, S5_REAL), jnp.float32)] * 3
        + [pltpu.VMEM((gp, S5_VEC + S5_REAL, S5_VEC), jnp.bfloat16)],
        compiler_params=pltpu.CompilerParams(
            dimension_semantics=("arbitrary",), vmem_limit_bytes=VMEM_LIMIT_BYTES),
        name="s5",
    )(u_grp, *weights)


def _merge_kernel(yg_ref, ap_ref, sb_ref, wglu_ref, bglu_ref, o_ref, ys_s):
    nchunk = MERGE_SUB // CHUNK_ROWS
    assert nchunk * BATCH == 16
    grows = MERGE_SUB // S5_CHUNK

    def step_major(a):
        a = a.reshape(nchunk, S5_CHUNK, BATCH, D_MODEL)
        return jnp.swapaxes(a, 0, 1).reshape(MERGE_SUB, D_MODEL)

    def chunk_major(a):
        a = a.reshape(S5_CHUNK, nchunk, BATCH, D_MODEL)
        return jnp.swapaxes(a, 0, 1).reshape(MERGE_SUB, D_MODEL)

    for sub in range(o_ref.shape[0] // MERGE_SUB):
        rs = slice(sub * MERGE_SUB, (sub + 1) * MERGE_SUB)
        gs = slice(sub * grows, (sub + 1) * grows)
        for tile in range(D_MODEL // LANES):
            ls = slice(tile * LANES, (tile + 1) * LANES)
            for half in range(S5_CHUNK // BLOCKS):
                w = [yg_ref[tile * BLOCKS + gi, gs, half * LANES:(half + 1) * LANES] for gi in range(BLOCKS)]
                v = _block_transpose(w)
                for s in range(BLOCKS):
                    step = half * BLOCKS + s
                    ys_s[sub, step * nchunk * BATCH:(step + 1) * nchunk * BATCH, ls] = (
                        _gelu2(v[s].astype(jnp.float32)).astype(jnp.bfloat16))

        ys = ys_s[sub]
        glu_a = _dot(ys, wglu_ref[:, 0:D_MODEL]) + bglu_ref[:, 0:D_MODEL]
        glu_b = _dot(ys, wglu_ref[:, D_MODEL:]) + bglu_ref[:, D_MODEL:]
        merged = (step_major(ap_ref[rs, :].astype(jnp.float32))
                  + step_major(sb_ref[rs, :].astype(jnp.float32)) * (glu_a * _sigmoid2(glu_b)))
        o_ref[rs, :] = chunk_major(merged).astype(o_ref.dtype)


def _merge_call(yg, ap, sb, layer, prm):
    rows = MERGE_ROWS
    weights = [prm[k] for k in ("wglu", "bglu")]
    return pl.pallas_call(
        _merge_kernel,
        grid=(ROWS // rows,),
        in_specs=[_group_spec(rows), _row_spec(rows, D_MODEL), _row_spec(rows, D_MODEL)]
        + [_layer_spec(w, layer) for w in weights],
        out_specs=_row_spec(rows, D_MODEL),
        out_shape=jax.ShapeDtypeStruct((ROWS, D_MODEL), jnp.bfloat16),
        scratch_shapes=[pltpu.VMEM((rows // MERGE_SUB, MERGE_SUB, D_MODEL), jnp.bfloat16)],
        compiler_params=pltpu.CompilerParams(
            dimension_semantics=("arbitrary",), vmem_limit_bytes=VMEM_LIMIT_BYTES),
        name="merge",
    )(yg, ap, sb, *weights)


def _ffn_kernel(x_ref, m_ref, p_ref, wo_ref, gffn_ref, wup_ref, cfw_ref, cfb_ref, wdn_ref,
                gple_ref, wpg_ref, wpp_ref, gfin_ref,
                o_ref, act_s, fc_s, pt_s, *, final):
    rows = x_ref.shape[0]
    halo = (FFN_CONV - 1) * BATCH

    @pl.when(pl.program_id(0) == 0)
    def _():
        fc_s[...] = jnp.zeros_like(fc_s)

    x1 = x_ref[...] + _dot(m_ref[...], wo_ref[...])
    h2 = _rmsnorm(x1, gffn_ref[...]).astype(jnp.bfloat16)

    def conv_block(col):
        up = _dot(h2, wup_ref[:, col:col + FFN_COLS])
        ext = jnp.concatenate([fc_s[:, col:col + FFN_COLS], up], axis=0)
        fc_s[:, col:col + FFN_COLS] = up[rows - halo:rows, :]
        out = cfb_ref[:, col:col + FFN_COLS] + cfw_ref[2:3, col:col + FFN_COLS] * up
        for k in range(FFN_CONV - 1):
            out = out + cfw_ref[k:k + 1, col:col + FFN_COLS] * ext[k * BATCH:k * BATCH + rows, :]
        return out

    for c in range(D_FF // FFN_COLS):
        col = c * FFN_COLS
        act = _gelu2(conv_block(col)) * conv_block(D_FF + col)
        act_s[:, col:col + FFN_COLS] = act.astype(jnp.bfloat16)

    x2 = x1 + _dot(act_s[...], wdn_ref[...])

    h3 = _rmsnorm(x2, gple_ref[...]).astype(jnp.bfloat16)
    gate = _sigmoid2(_dot(h3, wpg_ref[...]))
    _to_time_major(p_ref, pt_s)
    x3 = x2 + gate * _dot(pt_s[...].astype(jnp.bfloat16), wpp_ref[...])
    if final:
        x3 = _rmsnorm(x3, gfin_ref[...])
        _from_time_major(x3, o_ref)
    else:
        o_ref[...] = x3


def _ffn_call(xt, merged, p, layer, prm, final):
    rows = FFN_ROWS
    halo = (FFN_CONV - 1) * BATCH
    weights = [prm[k] for k in ("wo", "g_ffn", "wup", "cfw", "cfb", "wdn", "g_ple", "wpg", "wpp")]
    return pl.pallas_call(
        functools.partial(_ffn_kernel, final=final),
        grid=(ROWS // rows,),
        in_specs=[_row_spec(rows, D_MODEL), _row_spec(rows, D_MODEL),
                  pl.BlockSpec((None, BATCH, rows // BATCH, PLE_DIM), lambda i: (layer, 0, i, 0))]
        + [_layer_spec(w, layer) for w in weights] + [_layer_spec(prm["g_final"], 0)],
        out_specs=_batch_major_spec(rows, D_MODEL) if final else _row_spec(rows, D_MODEL),
        out_shape=jax.ShapeDtypeStruct((BATCH, SEQ, D_MODEL) if final else (ROWS, D_MODEL),
                                       jnp.float32),
        scratch_shapes=[
            pltpu.VMEM((rows, D_FF), jnp.bfloat16),
            pltpu.VMEM((halo, 2 * D_FF), jnp.float32),
            pltpu.VMEM((rows, PLE_DIM), jnp.float32),
        ],
        compiler_params=pltpu.CompilerParams(
            dimension_semantics=("arbitrary",), vmem_limit_bytes=VMEM_LIMIT_BYTES),
        name="ffn_final" if final else "ffn",
    )(xt, merged, p, *weights, prm["g_final"])


def _s5_params(a_re, a_im, log_dt, b_re, b_im, c_re, c_im, d):
    dt = jnp.exp(log_dt)[..., None]
    lr = a_re * dt
    li = a_im * dt
    mag = jnp.exp(lr)
    lb_re = mag * jnp.cos(li)
    lb_im = mag * jnp.sin(li)
    nr = lb_re - 1.0
    ni = lb_im
    den = a_re * a_re + a_im * a_im
    coef_re = (nr * a_re + ni * a_im) / den
    coef_im = (ni * a_re - nr * a_im) / den
    bb_re = coef_re[..., None] * b_re - coef_im[..., None] * b_im
    bb_im = coef_re[..., None] * b_im + coef_im[..., None] * b_re

    n = jnp.arange(S5_CHUNK + 1, dtype=jnp.float32)
    pmag = jnp.exp(lr[..., None] * n)
    pw_re = pmag * jnp.cos(li[..., None] * n)
    pw_im = pmag * jnp.sin(li[..., None] * n)

    ct_re = jnp.tile(jnp.swapaxes(c_re, -1, -2), (1, 1, 1, S5_CHUNK + 1))
    ct_im = jnp.tile(jnp.swapaxes(c_im, -1, -2), (1, 1, 1, S5_CHUNK + 1))
    pr_re = jnp.repeat(pw_re, S5_GROUP, axis=-1)
    pr_im = jnp.repeat(pw_im, S5_GROUP, axis=-1)
    cp_re = ct_re * pr_re - ct_im * pr_im
    cp_im = ct_re * pr_im + ct_im * pr_re
    lead = cp_re.shape[:3]
    cp0 = jnp.stack([cp_re[..., :S5_VEC], cp_im[..., :S5_VEC]], axis=2)
    cp1 = jnp.stack([cp_re[..., S5_GROUP:], cp_im[..., S5_GROUP:]], axis=2)

    bbt_re = jnp.swapaxes(bb_re, -1, -2)
    bbt_im = jnp.swapaxes(bb_im, -1, -2)
    bbt = jnp.stack([bbt_re, bbt_im], axis=2)

    rev_re = jnp.swapaxes(pw_re[..., S5_CHUNK - 1::-1], -1, -2)[..., None, :]
    rev_im = jnp.swapaxes(pw_im[..., S5_CHUNK - 1::-1], -1, -2)[..., None, :]
    wu_re = rev_re * bbt_re[..., None, :, :] - rev_im * bbt_im[..., None, :, :]
    wu_im = rev_re * bbt_im[..., None, :, :] + rev_im * bbt_re[..., None, :, :]
    wu = jnp.concatenate([wu_re, wu_im], axis=-1).reshape(lead[:2] + (S5_VEC, S5_REAL))

    dm = jnp.eye(S5_GROUP, dtype=jnp.float32) * d.reshape(lead[:2] + (1, S5_GROUP))
    dm = jnp.pad(dm, ((0, 0), (0, 0), (0, 0), (0, S5_VEC - S5_GROUP)))

    l_re, l_im = pw_re[..., S5_CHUNK], pw_im[..., S5_CHUNK]
    ar = jnp.concatenate([l_re, l_re], axis=-1)
    ai = jnp.concatenate([-l_im, l_im], axis=-1)
    coef = jnp.broadcast_to(jnp.stack([ar, ai], axis=2)[..., None, :],
                            lead[:2] + (2, BATCH, S5_REAL))
    return {"s5_wu": wu.astype(jnp.bfloat16), "s5_bbt": bbt, "s5_cp0": cp0, "s5_cp1": cp1,
            "s5_dm": dm, "s5_coef": coef}


def kernel(x, p, g_mix, w_in, conv_a_w, conv_a_b, gate_x_w, gate_x_b, gate_a_w, gate_a_b, lru_lambda, w_a_out, s5_a_re, s5_a_im, s5_log_dt, s5_b_re, s5_b_im, s5_c_re, s5_c_im, s5_d, w_glu, b_glu, w_o, g_ffn, w_up, conv_f_w, conv_f_b, w_down, g_ple, w_ple_gate, w_ple_proj, g_final):
    assert x.shape == (BATCH, SEQ, D_MODEL) and p.shape == (DEPTH, BATCH, SEQ, PLE_DIM)
    bf = jnp.bfloat16
    row = lambda v: v[:, None, :]
    ones = functools.partial(jnp.full, dtype=jnp.float32)
    in_scale = jnp.concatenate([ones((O_MA,), 1.0), ones((2 * D_MODEL,), 0.5)])
    glu_scale = jnp.concatenate([ones((D_MODEL,), 0.125), ones((D_MODEL,), 0.25)])
    prm = {
        "g_mix": row(g_mix), "g_ffn": row(g_ffn), "g_ple": row(g_ple), "g_final": g_final[None, None],
        "w_in": (w_in * in_scale).astype(bf), "cw": conv_a_w, "cb": row(conv_a_b),
        "gw": (0.5 * jnp.concatenate([gate_x_w, gate_a_w], axis=-1)).astype(bf),
        "gxb": row(0.5 * gate_x_b), "gab": row(0.5 * gate_a_b), "lam": row(lru_lambda),
        "wao": (0.125 * w_a_out).astype(bf),
        "wglu": (w_glu * glu_scale).astype(bf), "bglu": row(b_glu * (2.0 * glu_scale)),
        "wo": w_o.astype(bf),
        "wup": w_up.astype(bf), "cfw": conv_f_w, "cfb": row(conv_f_b),
        "wdn": (0.5 * w_down).astype(bf), "wpg": (0.5 * w_ple_gate).astype(bf),
        "wpp": (0.5 * w_ple_proj).astype(bf),
    }
    prm.update(_s5_params(s5_a_re, s5_a_im, s5_log_dt, s5_b_re, s5_b_im, s5_c_re, s5_c_im, s5_d))

    xt = x
    for i in range(DEPTH):
        if i == 0:
            ug, ap, sb, xt = _mixer_call(xt, i, prm, batch_major_in=True)
        else:
            ug, ap, sb = _mixer_call(xt, i, prm, batch_major_in=False)
        yg = _s5_call(ug, i, prm)
        merged = _merge_call(yg, ap, sb, i, prm)
        xt = _ffn_call(xt, merged, p, i, prm, final=(i == DEPTH - 1))
    return xt
```

```python
import functools
import math

import jax
import jax.numpy as jnp
from jax import lax
from jax.experimental import pallas as pl
from jax.experimental.pallas import tpu as pltpu

D_MODEL = 1024
BATCH = 8
SEQ = 4096
DEPTH = 2
PLE_DIM = 256
LRU_WIDTH = 1280
LRU_HEADS = 10
LRU_HEAD_DIM = 128
LRU_CONV = 4
LRU_C = 8.0
S5_GROUP = 16
S5_GROUPS = 64
S5_STATE = 64
D_FF = 3 * D_MODEL
FFN_CONV = 3
EPS = 1e-6

LANES = 128
ROWS = SEQ * BATCH
PAIR = 2 * LRU_HEAD_DIM
N_PAIRS = LRU_WIDTH // PAIR
S5_CHUNK = 16
S5_VEC = S5_CHUNK * S5_GROUP
S5_NCHUNK = SEQ // S5_CHUNK
S5_ROWS = S5_NCHUNK * BATCH
S5_REAL = 2 * S5_STATE
CHUNK_ROWS = S5_CHUNK * BATCH
BLOCKS = LANES // S5_GROUP

O_GB = LRU_WIDTH
O_UB = 2 * LRU_WIDTH
O_MA = O_UB + D_MODEL
O_MB = O_MA + D_MODEL

MIXER_ROWS = 512
MERGE_ROWS = 1024
MERGE_SUB = 256
FFN_ROWS = 1024
FFN_COLS = 512
S5_GROUPS_PER_STEP = 4
VMEM_LIMIT_BYTES = 60 * 1024 * 1024

_GELU_C = math.sqrt(2.0 / math.pi)


def _gelu2(x):
    return x * (1.0 + jnp.tanh(x * (_GELU_C + (_GELU_C * 0.044715) * (x * x))))


def _sigmoid2(half_x):
    return jnp.tanh(half_x) + 1.0


def _rmsnorm(x, g):
    ms = jnp.mean(x * x, axis=-1, keepdims=True)
    return x * lax.rsqrt(ms + EPS) * g


def _dot(a, b):
    return jnp.dot(a, b, preferred_element_type=jnp.float32)


def _layer_spec(w, layer):
    zeros = (0,) * (w.ndim - 1)
    return pl.BlockSpec((None,) + w.shape[1:], lambda i: (layer,) + zeros,
                        pipeline_mode=pl.Buffered(1))


def _row_spec(rows, cols):
    return pl.BlockSpec((rows, cols), lambda i: (i, 0))


def _batch_major_spec(rows, cols):
    return pl.BlockSpec((BATCH, rows // BATCH, cols), lambda i: (0, i, 0))


def _to_time_major(src_ref, dst_ref):
    for t in range(src_ref.shape[1]):
        dst_ref[t * BATCH:(t + 1) * BATCH, :] = src_ref[:, t, :]


def _from_time_major(src, dst_ref):
    for t in range(dst_ref.shape[1]):
        dst_ref[:, t, :] = src[t * BATCH:(t + 1) * BATCH, :]


def _group_spec(rows):
    return pl.BlockSpec((S5_GROUPS, rows // S5_CHUNK, S5_VEC), lambda i: (0, i, 0))


def _lane_roll(a, shift):
    if a.dtype == jnp.float32:
        return pltpu.roll(a, shift, axis=1)
    return jnp.concatenate([a[:, LANES - shift:], a[:, :LANES - shift]], axis=1)


def _block_transpose(v):
    blk = lax.broadcasted_iota(jnp.int32, v[0].shape, 1) // S5_GROUP
    for d in (4, 2, 1):
        bit = (blk & d) != 0
        out = list(v)
        for i in range(BLOCKS):
            if i & d == 0:
                lo, hi = v[i], v[i + d]
                out[i] = jnp.where(bit, _lane_roll(hi, S5_GROUP * d), lo)
                out[i + d] = jnp.where(bit, hi, _lane_roll(lo, LANES - S5_GROUP * d))
        v = out
    return v


def _mixer_kernel(x_ref, g_ref, win_ref, cw_ref, cb_ref, gw_ref, gxb_ref, gab_ref, lam_ref,
                  wao_ref, ug_ref, ap_ref, sb_ref, *rest, batch_major_in):
    if batch_major_in:
        xt_ref, ext_s, a_s, u_s, gb_s, ma_s, ub_s, hg_s, hc_s = rest
        _to_time_major(x_ref, xt_ref)
        x_ref = xt_ref
    else:
        ext_s, a_s, u_s, gb_s, ma_s, ub_s, hg_s, hc_s = rest
    rows = x_ref.shape[0]
    nchunk = rows // CHUNK_ROWS
    halo = (LRU_CONV - 1) * BATCH

    @pl.when(pl.program_id(0) == 0)
    def _():
        ext_s[0:halo, :] = jnp.zeros((halo, LRU_WIDTH), jnp.float32)
        hc_s[...] = jnp.zeros_like(hc_s)

    hb = _rmsnorm(x_ref[...], g_ref[...]).astype(jnp.bfloat16)

    lam = lam_ref[...]
    log_coef = (0.5 * LRU_C) * (jnp.minimum(lam, 0.0) - jnp.log1p(jnp.exp(-jnp.abs(lam))))

    for j in range(N_PAIRS):
        c0 = j * PAIR
        xa = _dot(hb, win_ref[:, c0:c0 + PAIR])
        ext_s[halo:halo + rows, c0:c0 + PAIR] = xa
        xc = cb_ref[:, c0:c0 + PAIR] + cw_ref[3:4, c0:c0 + PAIR] * xa
        for k in range(LRU_CONV - 1):
            xc = xc + cw_ref[k:k + 1, c0:c0 + PAIR] * ext_s[k * BATCH:k * BATCH + rows, c0:c0 + PAIR]
        xcb = xc.astype(jnp.bfloat16)
        for hh in range(2):
            lo = hh * LRU_HEAD_DIM
            cs = slice(c0 + lo, c0 + lo + LRU_HEAD_DIM)
            gates = _dot(xcb[:, lo:lo + LRU_HEAD_DIM], gw_ref[2 * j + hh])
            gx2 = _sigmoid2(gates[:, :LRU_HEAD_DIM] + gxb_ref[:, cs])
            log_a = log_coef[:, cs] * jnp.tanh(gates[:, LRU_HEAD_DIM:] + gab_ref[:, cs]) + log_coef[:, cs]
            a = jnp.exp(log_a)
            mult = jnp.sqrt(jnp.maximum((1.0 - a) * (1.0 + a), 1e-12))
            a_s[:, cs] = a
            u_s[:, cs] = mult * (gx2 * xc[:, lo:lo + LRU_HEAD_DIM])
        gb_s[:, c0:c0 + PAIR] = _gelu2(_dot(hb, win_ref[:, O_GB + c0:O_GB + c0 + PAIR]))

    ext_s[0:halo, :] = ext_s[rows:rows + halo, :]

    ma_s[...] = _sigmoid2(_dot(hb, win_ref[:, O_MA:O_MA + D_MODEL]))
    sb_ref[...] = _sigmoid2(_dot(hb, win_ref[:, O_MB:O_MB + D_MODEL])).astype(sb_ref.dtype)

    ub_s[...] = _dot(hb, win_ref[:, O_UB:O_UB + D_MODEL]).reshape(nchunk, S5_CHUNK, BATCH, D_MODEL)
    for tile in range(D_MODEL // LANES):
        for half in range(S5_CHUNK // BLOCKS):
            v = [ub_s[:, half * BLOCKS + s, :, tile * LANES:(tile + 1) * LANES].reshape(nchunk * BATCH, LANES)
                 for s in range(BLOCKS)]
            w = _block_transpose(v)
            for gi in range(BLOCKS):
                ug_ref[tile * BLOCKS + gi, :, half * LANES:(half + 1) * LANES] = w[gi].astype(ug_ref.dtype)

    h = hc_s[...]
    for t in range(rows // BATCH):
        r = t * BATCH
        h = a_s[r:r + BATCH, :] * h + u_s[r:r + BATCH, :]
        hg_s[r:r + BATCH, :] = h * gb_s[r:r + BATCH, :]
    hc_s[...] = h

    ya = _dot(hg_s[...].astype(jnp.bfloat16), wao_ref[...])
    ap_ref[...] = (ma_s[...] * ya).astype(ap_ref.dtype)


def _mixer_call(x, layer, prm, batch_major_in):
    rows = MIXER_ROWS
    halo = (LRU_CONV - 1) * BATCH
    out = jax.ShapeDtypeStruct((ROWS, D_MODEL), jnp.bfloat16)
    grp = jax.ShapeDtypeStruct((S5_GROUPS, S5_ROWS, S5_VEC), jnp.bfloat16)
    weights = [prm[k] for k in ("g_mix", "w_in", "cw", "cb", "gw", "gxb", "gab", "lam", "wao")]
    f32 = jnp.float32
    x_spec = _batch_major_spec(rows, D_MODEL) if batch_major_in else _row_spec(rows, D_MODEL)
    extra = batch_major_in * [jax.ShapeDtypeStruct((ROWS, D_MODEL), f32)]
    return pl.pallas_call(
        functools.partial(_mixer_kernel, batch_major_in=batch_major_in),
        grid=(ROWS // rows,),
        in_specs=[x_spec] + [_layer_spec(w, layer) for w in weights],
        out_specs=[_group_spec(rows)] + [_row_spec(rows, D_MODEL)] * (2 + len(extra)),
        out_shape=[grp, out, out] + extra,
        scratch_shapes=[
            pltpu.VMEM((halo + rows, LRU_WIDTH), f32),
            pltpu.VMEM((rows, LRU_WIDTH), f32),
            pltpu.VMEM((rows, LRU_WIDTH), f32),
            pltpu.VMEM((rows, LRU_WIDTH), f32),
            pltpu.VMEM((rows, D_MODEL), f32),
            pltpu.VMEM((rows // CHUNK_ROWS, S5_CHUNK, BATCH, D_MODEL), f32),
            pltpu.VMEM((rows, LRU_WIDTH), f32),
            pltpu.VMEM((BATCH, LRU_WIDTH), f32),
        ],
        compiler_params=pltpu.CompilerParams(
            dimension_semantics=("arbitrary",), vmem_limit_bytes=VMEM_LIMIT_BYTES),
        name="mixer_in" if batch_major_in else "mixer",
    )(x, *weights)


def _s5_kernel(u_ref, wu_ref, bbt_ref, cp0_ref, cp1_ref, dm_ref, coef_ref, y_ref,
               v_s, vsw_s, xs_s, tw_s):
    ngroups = u_ref.shape[0]
    hi = lax.Precision.HIGHEST
    lane = lax.broadcasted_iota(jnp.int32, (S5_GROUP, S5_VEC), 1)
    for g in range(ngroups):
        taps = (jnp.dot(bbt_ref[g, 0], cp0_ref[g, 0], precision=hi, preferred_element_type=jnp.float32)
                - jnp.dot(bbt_ref[g, 1], cp0_ref[g, 1], precision=hi, preferred_element_type=jnp.float32)
                + dm_ref[g])
        tw_s[g, 0:S5_GROUP, :] = taps.astype(jnp.bfloat16)
        for s in range(1, S5_CHUNK):
            shifted = jnp.where(lane >= s * S5_GROUP, pltpu.roll(taps, s * S5_GROUP, axis=1), 0.0)
            tw_s[g, s * S5_GROUP:(s + 1) * S5_GROUP, :] = shifted.astype(jnp.bfloat16)
        tw_s[g, S5_VEC:S5_VEC + S5_STATE, :] = cp1_ref[g, 0].astype(jnp.bfloat16)
        tw_s[g, S5_VEC + S5_STATE:, :] = (-cp1_ref[g, 1]).astype(jnp.bfloat16)

        v = _dot(u_ref[g], wu_ref[g])
        v_s[g] = v
        vsw_s[g] = pltpu.roll(v, S5_STATE, axis=1)

    zero = jnp.zeros((BATCH, S5_REAL), jnp.float32)
    state = [(zero, zero)] * ngroups
    for k in range(S5_NCHUNK):
        rs = slice(k * BATCH, (k + 1) * BATCH)
        for g in range(ngroups):
            x, xsw = state[g]
            xs_s[g, rs, :] = x
            ar = coef_ref[g, 0]
            ai = coef_ref[g, 1]
            state[g] = (ar * x + ai * xsw + v_s[g, rs, :], ar * xsw - ai * x + vsw_s[g, rs, :])

    for g in range(ngroups):
        y = _dot(u_ref[g], tw_s[g, 0:S5_VEC, :])
        y = y + _dot(xs_s[g].astype(jnp.bfloat16), tw_s[g, S5_VEC:, :])
        y_ref[g] = y.astype(y_ref.dtype)


def _s5_call(u_grp, layer, prm):
    gp = S5_GROUPS_PER_STEP
    weights = [prm[k] for k in ("s5_wu", "s5_bbt", "s5_cp0", "s5_cp1", "s5_dm", "s5_coef")]

    def wspec(w):
        zeros = (0,) * (w.ndim - 2)
        return pl.BlockSpec((None, gp) + w.shape[2:], lambda i: (layer, i) + zeros)

    blk = pl.BlockSpec((gp, S5_ROWS, S5_VEC), lambda i: (i, 0, 0))
    return pl.pallas_call(
        _s5_kernel,
        grid=(S5_GROUPS // gp,),
        in_specs=[blk] + [wspec(w) for w in weights],
        out_specs=blk,
        out_shape=jax.ShapeDtypeStruct((S5_GROUPS, S5_ROWS, S5_VEC), jnp.bfloat16),
        scratch_shapes=[pltpu.VMEM((gp, S5_ROWS, S5_REAL), jnp.float32)] * 3
        + [pltpu.VMEM((gp, S5_VEC + S5_REAL, S5_VEC), jnp.bfloat16)],
        compiler_params=pltpu.CompilerParams(
            dimension_semantics=("arbitrary",), vmem_limit_bytes=VMEM_LIMIT_BYTES),
        name="s5",
    )(u_grp, *weights)


def _merge_kernel(x_ref, yg_ref, ap_ref, sb_ref, wglu_ref, bglu_ref, wo_ref, o_ref, ys_s):
    nchunk = MERGE_SUB // CHUNK_ROWS
    assert nchunk * BATCH == 16
    grows = MERGE_SUB // S5_CHUNK

    def step_major(a):
        a = a.reshape(nchunk, S5_CHUNK, BATCH, D_MODEL)
        return jnp.swapaxes(a, 0, 1).reshape(MERGE_SUB, D_MODEL)

    def chunk_major(a):
        a = a.reshape(S5_CHUNK, nchunk, BATCH, D_MODEL)
        return jnp.swapaxes(a, 0, 1).reshape(MERGE_SUB, D_MODEL)

    for sub in range(x_ref.shape[0] // MERGE_SUB):
        rs = slice(sub * MERGE_SUB, (sub + 1) * MERGE_SUB)
        gs = slice(sub * grows, (sub + 1) * grows)
        for tile in range(D_MODEL // LANES):
            ls = slice(tile * LANES, (tile + 1) * LANES)
            for half in range(S5_CHUNK // BLOCKS):
                w = [yg_ref[tile * BLOCKS + gi, gs, half * LANES:(half + 1) * LANES] for gi in range(BLOCKS)]
                v = _block_transpose(w)
                for s in range(BLOCKS):
                    step = half * BLOCKS + s
                    ys_s[sub, step * nchunk * BATCH:(step + 1) * nchunk * BATCH, ls] = (
                        _gelu2(v[s].astype(jnp.float32)).astype(jnp.bfloat16))

        ys = ys_s[sub]
        glu_a = _dot(ys, wglu_ref[:, 0:D_MODEL]) + bglu_ref[:, 0:D_MODEL]
        glu_b = _dot(ys, wglu_ref[:, D_MODEL:]) + bglu_ref[:, D_MODEL:]
        merged = (step_major(ap_ref[rs, :].astype(jnp.float32))
                  + step_major(sb_ref[rs, :].astype(jnp.float32)) * (glu_a * _sigmoid2(glu_b)))
        o_ref[rs, :] = x_ref[rs, :] + chunk_major(_dot(merged.astype(jnp.bfloat16), wo_ref[...]))


def _merge_call(xt, yg, ap, sb, layer, prm):
    rows = MERGE_ROWS
    weights = [prm[k] for k in ("wglu", "bglu", "wo")]
    return pl.pallas_call(
        _merge_kernel,
        grid=(ROWS // rows,),
        in_specs=[_row_spec(rows, D_MODEL), _group_spec(rows), _row_spec(rows, D_MODEL),
                  _row_spec(rows, D_MODEL)] + [_layer_spec(w, layer) for w in weights],
        out_specs=_row_spec(rows, D_MODEL),
        out_shape=jax.ShapeDtypeStruct((ROWS, D_MODEL), jnp.float32),
        scratch_shapes=[pltpu.VMEM((rows // MERGE_SUB, MERGE_SUB, D_MODEL), jnp.bfloat16)],
        compiler_params=pltpu.CompilerParams(
            dimension_semantics=("arbitrary",), vmem_limit_bytes=VMEM_LIMIT_BYTES),
        name="merge",
    )(xt, yg, ap, sb, *weights)


def _ffn_kernel(x_ref, p_ref, gffn_ref, wup_ref, cfw_ref, cfb_ref, wdn_ref,
                gple_ref, wpg_ref, wpp_ref, gfin_ref,
                o_ref, act_s, fc_s, pt_s, *, final):
    rows = x_ref.shape[0]
    halo = (FFN_CONV - 1) * BATCH

    @pl.when(pl.program_id(0) == 0)
    def _():
        fc_s[...] = jnp.zeros_like(fc_s)

    x1 = x_ref[...]
    h2 = _rmsnorm(x1, gffn_ref[...]).astype(jnp.bfloat16)

    def conv_block(col):
        up = _dot(h2, wup_ref[:, col:col + FFN_COLS])
        ext = jnp.concatenate([fc_s[:, col:col + FFN_COLS], up], axis=0)
        fc_s[:, col:col + FFN_COLS] = up[rows - halo:rows, :]
        out = cfb_ref[:, col:col + FFN_COLS] + cfw_ref[2:3, col:col + FFN_COLS] * up
        for k in range(FFN_CONV - 1):
            out = out + cfw_ref[k:k + 1, col:col + FFN_COLS] * ext[k * BATCH:k * BATCH + rows, :]
        return out

    for c in range(D_FF // FFN_COLS):
        col = c * FFN_COLS
        act = _gelu2(conv_block(col)) * conv_block(D_FF + col)
        act_s[:, col:col + FFN_COLS] = act.astype(jnp.bfloat16)

    x2 = x1 + _dot(act_s[...], wdn_ref[...])

    h3 = _rmsnorm(x2, gple_ref[...]).astype(jnp.bfloat16)
    gate = _sigmoid2(_dot(h3, wpg_ref[...]))
    _to_time_major(p_ref, pt_s)
    x3 = x2 + gate * _dot(pt_s[...].astype(jnp.bfloat16), wpp_ref[...])
    if final:
        x3 = _rmsnorm(x3, gfin_ref[...])
        _from_time_major(x3, o_ref)
    else:
        o_ref[...] = x3


def _ffn_call(x1, p, layer, prm, final):
    rows = FFN_ROWS
    halo = (FFN_CONV - 1) * BATCH
    weights = [prm[k] for k in ("g_ffn", "wup", "cfw", "cfb", "wdn", "g_ple", "wpg", "wpp")]
    return pl.pallas_call(
        functools.partial(_ffn_kernel, final=final),
        grid=(ROWS // rows,),
        in_specs=[_row_spec(rows, D_MODEL),
                  pl.BlockSpec((None, BATCH, rows // BATCH, PLE_DIM), lambda i: (layer, 0, i, 0))]
        + [_layer_spec(w, layer) for w in weights] + [_layer_spec(prm["g_final"], 0)],
        out_specs=_batch_major_spec(rows, D_MODEL) if final else _row_spec(rows, D_MODEL),
        out_shape=jax.ShapeDtypeStruct((BATCH, SEQ, D_MODEL) if final else (ROWS, D_MODEL),
                                       jnp.float32),
        scratch_shapes=[
            pltpu.VMEM((rows, D_FF), jnp.bfloat16),
            pltpu.VMEM((halo, 2 * D_FF), jnp.float32),
            pltpu.VMEM((rows, PLE_DIM), jnp.float32),
        ],
        compiler_params=pltpu.CompilerParams(
            dimension_semantics=("arbitrary",), vmem_limit_bytes=VMEM_LIMIT_BYTES),
        name="ffn_final" if final else "ffn",
    )(x1, p, *weights, prm["g_final"])


def _s5_params(a_re, a_im, log_dt, b_re, b_im, c_re, c_im, d):
    dt = jnp.exp(log_dt)[..., None]
    lr = a_re * dt
    li = a_im * dt
    mag = jnp.exp(lr)
    lb_re = mag * jnp.cos(li)
    lb_im = mag * jnp.sin(li)
    nr = lb_re - 1.0
    ni = lb_im
    den = a_re * a_re + a_im * a_im
    coef_re = (nr * a_re + ni * a_im) / den
    coef_im = (ni * a_re - nr * a_im) / den
    bb_re = coef_re[..., None] * b_re - coef_im[..., None] * b_im
    bb_im = coef_re[..., None] * b_im + coef_im[..., None] * b_re

    n = jnp.arange(S5_CHUNK + 1, dtype=jnp.float32)
    pmag = jnp.exp(lr[..., None] * n)
    pw_re = pmag * jnp.cos(li[..., None] * n)
    pw_im = pmag * jnp.sin(li[..., None] * n)

    ct_re = jnp.tile(jnp.swapaxes(c_re, -1, -2), (1, 1, 1, S5_CHUNK + 1))
    ct_im = jnp.tile(jnp.swapaxes(c_im, -1, -2), (1, 1, 1, S5_CHUNK + 1))
    pr_re = jnp.repeat(pw_re, S5_GROUP, axis=-1)
    pr_im = jnp.repeat(pw_im, S5_GROUP, axis=-1)
    cp_re = ct_re * pr_re - ct_im * pr_im
    cp_im = ct_re * pr_im + ct_im * pr_re
    lead = cp_re.shape[:3]
    cp0 = jnp.stack([cp_re[..., :S5_VEC], cp_im[..., :S5_VEC]], axis=2)
    cp1 = jnp.stack([cp_re[..., S5_GROUP:], cp_im[..., S5_GROUP:]], axis=2)

    bbt_re = jnp.swapaxes(bb_re, -1, -2)
    bbt_im = jnp.swapaxes(bb_im, -1, -2)
    bbt = jnp.stack([bbt_re, bbt_im], axis=2)

    rev_re = jnp.swapaxes(pw_re[..., S5_CHUNK - 1::-1], -1, -2)[..., None, :]
    rev_im = jnp.swapaxes(pw_im[..., S5_CHUNK - 1::-1], -1, -2)[..., None, :]
    wu_re = rev_re * bbt_re[..., None, :, :] - rev_im * bbt_im[..., None, :, :]
    wu_im = rev_re * bbt_im[..., None, :, :] + rev_im * bbt_re[..., None, :, :]
    wu = jnp.concatenate([wu_re, wu_im], axis=-1).reshape(lead[:2] + (S5_VEC, S5_REAL))

    dm = jnp.eye(S5_GROUP, dtype=jnp.float32) * d.reshape(lead[:2] + (1, S5_GROUP))
    dm = jnp.pad(dm, ((0, 0), (0, 0), (0, 0), (0, S5_VEC - S5_GROUP)))

    l_re, l_im = pw_re[..., S5_CHUNK], pw_im[..., S5_CHUNK]
    ar = jnp.concatenate([l_re, l_re], axis=-1)
    ai = jnp.concatenate([-l_im, l_im], axis=-1)
    coef = jnp.broadcast_to(jnp.stack([ar, ai], axis=2)[..., None, :],
                            lead[:2] + (2, BATCH, S5_REAL))
    return {"s5_wu": wu.astype(jnp.bfloat16), "s5_bbt": bbt, "s5_cp0": cp0, "s5_cp1": cp1,
            "s5_dm": dm, "s5_coef": coef}


def kernel(x, p, g_mix, w_in, conv_a_w, conv_a_b, gate_x_w, gate_x_b, gate_a_w, gate_a_b, lru_lambda, w_a_out, s5_a_re, s5_a_im, s5_log_dt, s5_b_re, s5_b_im, s5_c_re, s5_c_im, s5_d, w_glu, b_glu, w_o, g_ffn, w_up, conv_f_w, conv_f_b, w_down, g_ple, w_ple_gate, w_ple_proj, g_final):
    assert x.shape == (BATCH, SEQ, D_MODEL) and p.shape == (DEPTH, BATCH, SEQ, PLE_DIM)
    bf = jnp.bfloat16
    row = lambda v: v[:, None, :]
    ones = functools.partial(jnp.full, dtype=jnp.float32)
    in_scale = jnp.concatenate([ones((O_MA,), 1.0), ones((2 * D_MODEL,), 0.5)])
    glu_scale = jnp.concatenate([ones((D_MODEL,), 0.125), ones((D_MODEL,), 0.25)])
    prm = {
        "g_mix": row(g_mix), "g_ffn": row(g_ffn), "g_ple": row(g_ple), "g_final": g_final[None, None],
        "w_in": (w_in * in_scale).astype(bf), "cw": conv_a_w, "cb": row(conv_a_b),
        "gw": (0.5 * jnp.concatenate([gate_x_w, gate_a_w], axis=-1)).astype(bf),
        "gxb": row(0.5 * gate_x_b), "gab": row(0.5 * gate_a_b), "lam": row(lru_lambda),
        "wao": (0.125 * w_a_out).astype(bf),
        "wglu": (w_glu * glu_scale).astype(bf), "bglu": row(b_glu * (2.0 * glu_scale)),
        "wo": w_o.astype(bf),
        "wup": w_up.astype(bf), "cfw": conv_f_w, "cfb": row(conv_f_b),
        "wdn": (0.5 * w_down).astype(bf), "wpg": (0.5 * w_ple_gate).astype(bf),
        "wpp": (0.5 * w_ple_proj).astype(bf),
    }
    prm.update(_s5_params(s5_a_re, s5_a_im, s5_log_dt, s5_b_re, s5_b_im, s5_c_re, s5_c_im, s5_d))

    xt = x
    for i in range(DEPTH):
        if i == 0:
            ug, ap, sb, xt = _mixer_call(xt, i, prm, batch_major_in=True)
        else:
            ug, ap, sb = _mixer_call(xt, i, prm, batch_major_in=False)
        yg = _s5_call(ug, i, prm)
        x1 = _merge_call(xt, yg, ap, sb, i, prm)
        xt = _ffn_call(x1, p, i, prm, final=(i == DEPTH - 1))
    return xt
```

```python
import functools
import math

import jax
import jax.numpy as jnp
from jax import lax
from jax.experimental import pallas as pl
from jax.experimental.pallas import tpu as pltpu

D_MODEL = 1024
BATCH = 8
SEQ = 4096
DEPTH = 2
PLE_DIM = 256
LRU_WIDTH = 1280
LRU_HEADS = 10
LRU_HEAD_DIM = 128
LRU_CONV = 4
LRU_C = 8.0
S5_GROUP = 16
S5_GROUPS = 64
S5_STATE = 64
D_FF = 3 * D_MODEL
FFN_CONV = 3
EPS = 1e-6

LANES = 128
ROWS = SEQ * BATCH
PAIR = 2 * LRU_HEAD_DIM
N_PAIRS = LRU_WIDTH // PAIR
S5_CHUNK = 16
S5_VEC = S5_CHUNK * S5_GROUP
S5_NCHUNK = SEQ // S5_CHUNK
S5_ROWS = S5_NCHUNK * BATCH
S5_REAL = 2 * S5_STATE
CHUNK_ROWS = S5_CHUNK * BATCH
BLOCKS = LANES // S5_GROUP

O_GB = LRU_WIDTH
O_UB = 2 * LRU_WIDTH
O_MA = O_UB + D_MODEL
O_MB = O_MA + D_MODEL

MIXER_ROWS = 512
MERGE_ROWS = 1024
MERGE_SUB = 256
FFN_ROWS = 1024
FFN_COLS = 256
S5_GROUPS_PER_STEP = 4
VMEM_LIMIT_BYTES = 60 * 1024 * 1024

_GELU_C = math.sqrt(2.0 / math.pi)


def _gelu2(x):
    return x * (1.0 + jnp.tanh(x * (_GELU_C + (_GELU_C * 0.044715) * (x * x))))


def _sigmoid2(half_x):
    return jnp.tanh(half_x) + 1.0


def _rmsnorm(x, g):
    ms = jnp.mean(x * x, axis=-1, keepdims=True)
    return x * lax.rsqrt(ms + EPS) * g


def _dot(a, b):
    return jnp.dot(a, b, preferred_element_type=jnp.float32)


def _layer_spec(w, layer):
    zeros = (0,) * (w.ndim - 1)
    return pl.BlockSpec((None,) + w.shape[1:], lambda i: (layer,) + zeros,
                        pipeline_mode=pl.Buffered(1))


def _row_spec(rows, cols):
    return pl.BlockSpec((rows, cols), lambda i: (i, 0))


def _batch_major_spec(rows, cols):
    return pl.BlockSpec((BATCH, rows // BATCH, cols), lambda i: (0, i, 0))


def _to_time_major(src_ref, dst_ref):
    for t in range(src_ref.shape[1]):
        dst_ref[t * BATCH:(t + 1) * BATCH, :] = src_ref[:, t, :]


def _from_time_major(src, dst_ref):
    for t in range(dst_ref.shape[1]):
        dst_ref[:, t, :] = src[t * BATCH:(t + 1) * BATCH, :]


def _group_spec(rows):
    return pl.BlockSpec((S5_GROUPS, rows // S5_CHUNK, S5_VEC), lambda i: (0, i, 0))


def _lane_roll(a, shift):
    if a.dtype == jnp.float32:
        return pltpu.roll(a, shift, axis=1)
    return jnp.concatenate([a[:, LANES - shift:], a[:, :LANES - shift]], axis=1)


def _block_transpose(v):
    blk = lax.broadcasted_iota(jnp.int32, v[0].shape, 1) // S5_GROUP
    for d in (4, 2, 1):
        bit = (blk & d) != 0
        out = list(v)
        for i in range(BLOCKS):
            if i & d == 0:
                lo, hi = v[i], v[i + d]
                out[i] = jnp.where(bit, _lane_roll(hi, S5_GROUP * d), lo)
                out[i + d] = jnp.where(bit, hi, _lane_roll(lo, LANES - S5_GROUP * d))
        v = out
    return v


def _mixer_kernel(x_ref, g_ref, win_ref, cw_ref, cb_ref, gw_ref, gxb_ref, gab_ref, lam_ref,
                  wao_ref, ug_ref, ap_ref, sb_ref, *rest, batch_major_in):
    if batch_major_in:
        xt_ref, ext_s, a_s, u_s, gb_s, ma_s, ub_s, hg_s, hc_s = rest
        _to_time_major(x_ref, xt_ref)
        x_ref = xt_ref
    else:
        ext_s, a_s, u_s, gb_s, ma_s, ub_s, hg_s, hc_s = rest
    rows = x_ref.shape[0]
    nchunk = rows // CHUNK_ROWS
    halo = (LRU_CONV - 1) * BATCH

    @pl.when(pl.program_id(0) == 0)
    def _():
        ext_s[0:halo, :] = jnp.zeros((halo, LRU_WIDTH), jnp.float32)
        hc_s[...] = jnp.zeros_like(hc_s)

    hb = _rmsnorm(x_ref[...], g_ref[...]).astype(jnp.bfloat16)

    lam = lam_ref[...]
    log_coef = (0.5 * LRU_C) * (jnp.minimum(lam, 0.0) - jnp.log1p(jnp.exp(-jnp.abs(lam))))

    for j in range(N_PAIRS):
        c0 = j * PAIR
        xa = _dot(hb, win_ref[:, c0:c0 + PAIR])
        ext_s[halo:halo + rows, c0:c0 + PAIR] = xa
        xc = cb_ref[:, c0:c0 + PAIR] + cw_ref[3:4, c0:c0 + PAIR] * xa
        for k in range(LRU_CONV - 1):
            xc = xc + cw_ref[k:k + 1, c0:c0 + PAIR] * ext_s[k * BATCH:k * BATCH + rows, c0:c0 + PAIR]
        xcb = xc.astype(jnp.bfloat16)
        for hh in range(2):
            lo = hh * LRU_HEAD_DIM
            cs = slice(c0 + lo, c0 + lo + LRU_HEAD_DIM)
            gates = _dot(xcb[:, lo:lo + LRU_HEAD_DIM], gw_ref[2 * j + hh])
            gx2 = _sigmoid2(gates[:, :LRU_HEAD_DIM] + gxb_ref[:, cs])
            log_a = log_coef[:, cs] * jnp.tanh(gates[:, LRU_HEAD_DIM:] + gab_ref[:, cs]) + log_coef[:, cs]
            a = jnp.exp(log_a)
            mult = jnp.sqrt(jnp.maximum((1.0 - a) * (1.0 + a), 1e-12))
            a_s[:, cs] = a
            u_s[:, cs] = mult * (gx2 * xc[:, lo:lo + LRU_HEAD_DIM])
        gb_s[:, c0:c0 + PAIR] = _gelu2(_dot(hb, win_ref[:, O_GB + c0:O_GB + c0 + PAIR]))

    ext_s[0:halo, :] = ext_s[rows:rows + halo, :]

    ma_s[...] = _sigmoid2(_dot(hb, win_ref[:, O_MA:O_MA + D_MODEL]))
    sb_ref[...] = _sigmoid2(_dot(hb, win_ref[:, O_MB:O_MB + D_MODEL])).astype(sb_ref.dtype)

    ub_s[...] = _dot(hb, win_ref[:, O_UB:O_UB + D_MODEL]).reshape(nchunk, S5_CHUNK, BATCH, D_MODEL)
    for tile in range(D_MODEL // LANES):
        for half in range(S5_CHUNK // BLOCKS):
            v = [ub_s[:, half * BLOCKS + s, :, tile * LANES:(tile + 1) * LANES].reshape(nchunk * BATCH, LANES)
                 for s in range(BLOCKS)]
            w = _block_transpose(v)
            for gi in range(BLOCKS):
                ug_ref[tile * BLOCKS + gi, :, half * LANES:(half + 1) * LANES] = w[gi].astype(ug_ref.dtype)

    h = hc_s[...]
    for t in range(rows // BATCH):
        r = t * BATCH
        h = a_s[r:r + BATCH, :] * h + u_s[r:r + BATCH, :]
        hg_s[r:r + BATCH, :] = h * gb_s[r:r + BATCH, :]
    hc_s[...] = h

    ya = _dot(hg_s[...].astype(jnp.bfloat16), wao_ref[...])
    ap_ref[...] = (ma_s[...] * ya).astype(ap_ref.dtype)


def _mixer_call(x, layer, prm, batch_major_in):
    rows = MIXER_ROWS
    halo = (LRU_CONV - 1) * BATCH
    out = jax.ShapeDtypeStruct((ROWS, D_MODEL), jnp.bfloat16)
    grp = jax.ShapeDtypeStruct((S5_GROUPS, S5_ROWS, S5_VEC), jnp.bfloat16)
    weights = [prm[k] for k in ("g_mix", "w_in", "cw", "cb", "gw", "gxb", "gab", "lam", "wao")]
    f32 = jnp.float32
    x_spec = _batch_major_spec(rows, D_MODEL) if batch_major_in else _row_spec(rows, D_MODEL)
    extra = batch_major_in * [jax.ShapeDtypeStruct((ROWS, D_MODEL), f32)]
    return pl.pallas_call(
        functools.partial(_mixer_kernel, batch_major_in=batch_major_in),
        grid=(ROWS // rows,),
        in_specs=[x_spec] + [_layer_spec(w, layer) for w in weights],
        out_specs=[_group_spec(rows)] + [_row_spec(rows, D_MODEL)] * (2 + len(extra)),
        out_shape=[grp, out, out] + extra,
        scratch_shapes=[
            pltpu.VMEM((halo + rows, LRU_WIDTH), f32),
            pltpu.VMEM((rows, LRU_WIDTH), f32),
            pltpu.VMEM((rows, LRU_WIDTH), f32),
            pltpu.VMEM((rows, LRU_WIDTH), f32),
            pltpu.VMEM((rows, D_MODEL), f32),
            pltpu.VMEM((rows // CHUNK_ROWS, S5_CHUNK, BATCH, D_MODEL), f32),
            pltpu.VMEM((rows, LRU_WIDTH), f32),
            pltpu.VMEM((BATCH, LRU_WIDTH), f32),
        ],
        compiler_params=pltpu.CompilerParams(
            dimension_semantics=("arbitrary",), vmem_limit_bytes=VMEM_LIMIT_BYTES),
        name="mixer_in" if batch_major_in else "mixer",
    )(x, *weights)


def _s5_kernel(u_ref, wu_ref, bbt_ref, cp0_ref, cp1_ref, dm_ref, coef_ref, y_ref,
               v_s, vsw_s, xs_s, tw_s):
    ngroups = u_ref.shape[0]
    hi = lax.Precision.HIGHEST
    lane = lax.broadcasted_iota(jnp.int32, (S5_GROUP, S5_VEC), 1)
    for g in range(ngroups):
        taps = (jnp.dot(bbt_ref[g, 0], cp0_ref[g, 0], precision=hi, preferred_element_type=jnp.float32)
                - jnp.dot(bbt_ref[g, 1], cp0_ref[g, 1], precision=hi, preferred_element_type=jnp.float32)
                + dm_ref[g])
        tw_s[g, 0:S5_GROUP, :] = taps.astype(jnp.bfloat16)
        for s in range(1, S5_CHUNK):
            shifted = jnp.where(lane >= s * S5_GROUP, pltpu.roll(taps, s * S5_GROUP, axis=1), 0.0)
            tw_s[g, s * S5_GROUP:(s + 1) * S5_GROUP, :] = shifted.astype(jnp.bfloat16)
        tw_s[g, S5_VEC:S5_VEC + S5_STATE, :] = cp1_ref[g, 0].astype(jnp.bfloat16)
        tw_s[g, S5_VEC + S5_STATE:, :] = (-cp1_ref[g, 1]).astype(jnp.bfloat16)

        v = _dot(u_ref[g], wu_ref[g])
        v_s[g] = v
        vsw_s[g] = pltpu.roll(v, S5_STATE, axis=1)

    zero = jnp.zeros((BATCH, S5_REAL), jnp.float32)
    state = [(zero, zero)] * ngroups
    for k in range(S5_NCHUNK):
        rs = slice(k * BATCH, (k + 1) * BATCH)
        for g in range(ngroups):
            x, xsw = state[g]
            xs_s[g, rs, :] = x
            ar = coef_ref[g, 0]
            ai = coef_ref[g, 1]
            state[g] = (ar * x + ai * xsw + v_s[g, rs, :], ar * xsw - ai * x + vsw_s[g, rs, :])

    for g in range(ngroups):
        y = _dot(u_ref[g], tw_s[g, 0:S5_VEC, :])
        y = y + _dot(xs_s[g].astype(jnp.bfloat16), tw_s[g, S5_VEC:, :])
        y_ref[g] = y.astype(y_ref.dtype)


def _s5_call(u_grp, layer, prm):
    gp = S5_GROUPS_PER_STEP
    weights = [prm[k] for k in ("s5_wu", "s5_bbt", "s5_cp0", "s5_cp1", "s5_dm", "s5_coef")]

    def wspec(w):
        zeros = (0,) * (w.ndim - 2)
        return pl.BlockSpec((None, gp) + w.shape[2:], lambda i: (layer, i) + zeros)

    blk = pl.BlockSpec((gp, S5_ROWS, S5_VEC), lambda i: (i, 0, 0))
    return pl.pallas_call(
        _s5_kernel,
        grid=(S5_GROUPS // gp,),
        in_specs=[blk] + [wspec(w) for w in weights],
        out_specs=blk,
        out_shape=jax.ShapeDtypeStruct((S5_GROUPS, S5_ROWS, S5_VEC), jnp.bfloat16),
        scratch_shapes=[pltpu.VMEM((gp, S5_ROWS, S5_REAL), jnp.float32)] * 3
        + [pltpu.VMEM((gp, S5_VEC + S5_REAL, S5_VEC), jnp.bfloat16)],
        compiler_params=pltpu.CompilerParams(
            dimension_semantics=("arbitrary",), vmem_limit_bytes=VMEM_LIMIT_BYTES),
        name="s5",
    )(u_grp, *weights)


def _merge_kernel(x_ref, yg_ref, ap_ref, sb_ref, wglu_ref, bglu_ref, wo_ref, o_ref, ys_s):
    nchunk = MERGE_SUB // CHUNK_ROWS
    assert nchunk * BATCH == 16
    grows = MERGE_SUB // S5_CHUNK

    def step_major(a):
        a = a.reshape(nchunk, S5_CHUNK, BATCH, D_MODEL)
        return jnp.swapaxes(a, 0, 1).reshape(MERGE_SUB, D_MODEL)

    def chunk_major(a):
        a = a.reshape(S5_CHUNK, nchunk, BATCH, D_MODEL)
        return jnp.swapaxes(a, 0, 1).reshape(MERGE_SUB, D_MODEL)

    for sub in range(x_ref.shape[0] // MERGE_SUB):
        rs = slice(sub * MERGE_SUB, (sub + 1) * MERGE_SUB)
        gs = slice(sub * grows, (sub + 1) * grows)
        for tile in range(D_MODEL // LANES):
            ls = slice(tile * LANES, (tile + 1) * LANES)
            for half in range(S5_CHUNK // BLOCKS):
                w = [yg_ref[tile * BLOCKS + gi, gs, half * LANES:(half + 1) * LANES] for gi in range(BLOCKS)]
                v = _block_transpose(w)
                for s in range(BLOCKS):
                    step = half * BLOCKS + s
                    ys_s[sub, step * nchunk * BATCH:(step + 1) * nchunk * BATCH, ls] = (
                        _gelu2(v[s].astype(jnp.float32)).astype(jnp.bfloat16))

        ys = ys_s[sub]
        glu_a = _dot(ys, wglu_ref[:, 0:D_MODEL]) + bglu_ref[:, 0:D_MODEL]
        glu_b = _dot(ys, wglu_ref[:, D_MODEL:]) + bglu_ref[:, D_MODEL:]
        merged = (step_major(ap_ref[rs, :].astype(jnp.float32))
                  + step_major(sb_ref[rs, :].astype(jnp.float32)) * (glu_a * _sigmoid2(glu_b)))
        o_ref[rs, :] = x_ref[rs, :] + chunk_major(_dot(merged.astype(jnp.bfloat16), wo_ref[...]))


def _merge_call(xt, yg, ap, sb, layer, prm):
    rows = MERGE_ROWS
    weights = [prm[k] for k in ("wglu", "bglu", "wo")]
    return pl.pallas_call(
        _merge_kernel,
        grid=(ROWS // rows,),
        in_specs=[_row_spec(rows, D_MODEL), _group_spec(rows), _row_spec(rows, D_MODEL),
                  _row_spec(rows, D_MODEL)] + [_layer_spec(w, layer) for w in weights],
        out_specs=_row_spec(rows, D_MODEL),
        out_shape=jax.ShapeDtypeStruct((ROWS, D_MODEL), jnp.float32),
        scratch_shapes=[pltpu.VMEM((rows // MERGE_SUB, MERGE_SUB, D_MODEL), jnp.bfloat16)],
        compiler_params=pltpu.CompilerParams(
            dimension_semantics=("arbitrary",), vmem_limit_bytes=VMEM_LIMIT_BYTES),
        name="merge",
    )(xt, yg, ap, sb, *weights)


def _ffn_kernel(x_ref, p_ref, gffn_ref, wup_ref, cfw_ref, cfb_ref, wdn_ref,
                gple_ref, wpg_ref, wpp_ref, gfin_ref,
                o_ref, act_s, fc_s, pt_s, *, final):
    rows = x_ref.shape[0]
    halo = (FFN_CONV - 1) * BATCH

    @pl.when(pl.program_id(0) == 0)
    def _():
        fc_s[...] = jnp.zeros_like(fc_s)

    x1 = x_ref[...]
    h2 = _rmsnorm(x1, gffn_ref[...]).astype(jnp.bfloat16)

    def conv_block(col):
        up = _dot(h2, wup_ref[:, col:col + FFN_COLS])
        ext = jnp.concatenate([fc_s[:, col:col + FFN_COLS], up], axis=0)
        fc_s[:, col:col + FFN_COLS] = up[rows - halo:rows, :]
        out = cfb_ref[:, col:col + FFN_COLS] + cfw_ref[2:3, col:col + FFN_COLS] * up
        for k in range(FFN_CONV - 1):
            out = out + cfw_ref[k:k + 1, col:col + FFN_COLS] * ext[k * BATCH:k * BATCH + rows, :]
        return out

    for c in range(D_FF // FFN_COLS):
        col = c * FFN_COLS
        act = _gelu2(conv_block(col)) * conv_block(D_FF + col)
        act_s[:, col:col + FFN_COLS] = act.astype(jnp.bfloat16)

    x2 = x1 + _dot(act_s[...], wdn_ref[...])

    h3 = _rmsnorm(x2, gple_ref[...]).astype(jnp.bfloat16)
    gate = _sigmoid2(_dot(h3, wpg_ref[...]))
    _to_time_major(p_ref, pt_s)
    x3 = x2 + gate * _dot(pt_s[...].astype(jnp.bfloat16), wpp_ref[...])
    if final:
        x3 = _rmsnorm(x3, gfin_ref[...])
        _from_time_major(x3, o_ref)
    else:
        o_ref[...] = x3


def _ffn_call(x1, p, layer, prm, final):
    rows = FFN_ROWS
    halo = (FFN_CONV - 1) * BATCH
    weights = [prm[k] for k in ("g_ffn", "wup", "cfw", "cfb", "wdn", "g_ple", "wpg", "wpp")]
    return pl.pallas_call(
        functools.partial(_ffn_kernel, final=final),
        grid=(ROWS // rows,),
        in_specs=[_row_spec(rows, D_MODEL),
                  pl.BlockSpec((None, BATCH, rows // BATCH, PLE_DIM), lambda i: (layer, 0, i, 0))]
        + [_layer_spec(w, layer) for w in weights] + [_layer_spec(prm["g_final"], 0)],
        out_specs=_batch_major_spec(rows, D_MODEL) if final else _row_spec(rows, D_MODEL),
        out_shape=jax.ShapeDtypeStruct((BATCH, SEQ, D_MODEL) if final else (ROWS, D_MODEL),
                                       jnp.float32),
        scratch_shapes=[
            pltpu.VMEM((rows, D_FF), jnp.bfloat16),
            pltpu.VMEM((halo, 2 * D_FF), jnp.float32),
            pltpu.VMEM((rows, PLE_DIM), jnp.float32),
        ],
        compiler_params=pltpu.CompilerParams(
            dimension_semantics=("arbitrary",), vmem_limit_bytes=VMEM_LIMIT_BYTES),
        name="ffn_final" if final else "ffn",
    )(x1, p, *weights, prm["g_final"])


def _s5_params(a_re, a_im, log_dt, b_re, b_im, c_re, c_im, d):
    dt = jnp.exp(log_dt)[..., None]
    lr = a_re * dt
    li = a_im * dt
    mag = jnp.exp(lr)
    lb_re = mag * jnp.cos(li)
    lb_im = mag * jnp.sin(li)
    nr = lb_re - 1.0
    ni = lb_im
    den = a_re * a_re + a_im * a_im
    coef_re = (nr * a_re + ni * a_im) / den
    coef_im = (ni * a_re - nr * a_im) / den
    bb_re = coef_re[..., None] * b_re - coef_im[..., None] * b_im
    bb_im = coef_re[..., None] * b_im + coef_im[..., None] * b_re

    n = jnp.arange(S5_CHUNK + 1, dtype=jnp.float32)
    pmag = jnp.exp(lr[..., None] * n)
    pw_re = pmag * jnp.cos(li[..., None] * n)
    pw_im = pmag * jnp.sin(li[..., None] * n)

    ct_re = jnp.tile(jnp.swapaxes(c_re, -1, -2), (1, 1, 1, S5_CHUNK + 1))
    ct_im = jnp.tile(jnp.swapaxes(c_im, -1, -2), (1, 1, 1, S5_CHUNK + 1))
    pr_re = jnp.repeat(pw_re, S5_GROUP, axis=-1)
    pr_im = jnp.repeat(pw_im, S5_GROUP, axis=-1)
    cp_re = ct_re * pr_re - ct_im * pr_im
    cp_im = ct_re * pr_im + ct_im * pr_re
    lead = cp_re.shape[:3]
    cp0 = jnp.stack([cp_re[..., :S5_VEC], cp_im[..., :S5_VEC]], axis=2)
    cp1 = jnp.stack([cp_re[..., S5_GROUP:], cp_im[..., S5_GROUP:]], axis=2)

    bbt_re = jnp.swapaxes(bb_re, -1, -2)
    bbt_im = jnp.swapaxes(bb_im, -1, -2)
    bbt = jnp.stack([bbt_re, bbt_im], axis=2)

    rev_re = jnp.swapaxes(pw_re[..., S5_CHUNK - 1::-1], -1, -2)[..., None, :]
    rev_im = jnp.swapaxes(pw_im[..., S5_CHUNK - 1::-1], -1, -2)[..., None, :]
    wu_re = rev_re * bbt_re[..., None, :, :] - rev_im * bbt_im[..., None, :, :]
    wu_im = rev_re * bbt_im[..., None, :, :] + rev_im * bbt_re[..., None, :, :]
    wu = jnp.concatenate([wu_re, wu_im], axis=-1).reshape(lead[:2] + (S5_VEC, S5_REAL))

    dm = jnp.eye(S5_GROUP, dtype=jnp.float32) * d.reshape(lead[:2] + (1, S5_GROUP))
    dm = jnp.pad(dm, ((0, 0), (0, 0), (0, 0), (0, S5_VEC - S5_GROUP)))

    l_re, l_im = pw_re[..., S5_CHUNK], pw_im[..., S5_CHUNK]
    ar = jnp.concatenate([l_re, l_re], axis=-1)
    ai = jnp.concatenate([-l_im, l_im], axis=-1)
    coef = jnp.broadcast_to(jnp.stack([ar, ai], axis=2)[..., None, :],
                            lead[:2] + (2, BATCH, S5_REAL))
    return {"s5_wu": wu.astype(jnp.bfloat16), "s5_bbt": bbt, "s5_cp0": cp0, "s5_cp1": cp1,
            "s5_dm": dm, "s5_coef": coef}


def kernel(x, p, g_mix, w_in, conv_a_w, conv_a_b, gate_x_w, gate_x_b, gate_a_w, gate_a_b, lru_lambda, w_a_out, s5_a_re, s5_a_im, s5_log_dt, s5_b_re, s5_b_im, s5_c_re, s5_c_im, s5_d, w_glu, b_glu, w_o, g_ffn, w_up, conv_f_w, conv_f_b, w_down, g_ple, w_ple_gate, w_ple_proj, g_final):
    assert x.shape == (BATCH, SEQ, D_MODEL) and p.shape == (DEPTH, BATCH, SEQ, PLE_DIM)
    bf = jnp.bfloat16
    row = lambda v: v[:, None, :]
    ones = functools.partial(jnp.full, dtype=jnp.float32)
    in_scale = jnp.concatenate([ones((O_MA,), 1.0), ones((2 * D_MODEL,), 0.5)])
    glu_scale = jnp.concatenate([ones((D_MODEL,), 0.125), ones((D_MODEL,), 0.25)])
    prm = {
        "g_mix": row(g_mix), "g_ffn": row(g_ffn), "g_ple": row(g_ple), "g_final": g_final[None, None],
        "w_in": (w_in * in_scale).astype(bf), "cw": conv_a_w, "cb": row(conv_a_b),
        "gw": (0.5 * jnp.concatenate([gate_x_w, gate_a_w], axis=-1)).astype(bf),
        "gxb": row(0.5 * gate_x_b), "gab": row(0.5 * gate_a_b), "lam": row(lru_lambda),
        "wao": (0.125 * w_a_out).astype(bf),
        "wglu": (w_glu * glu_scale).astype(bf), "bglu": row(b_glu * (2.0 * glu_scale)),
        "wo": w_o.astype(bf),
        "wup": w_up.astype(bf), "cfw": conv_f_w, "cfb": row(conv_f_b),
        "wdn": (0.5 * w_down).astype(bf), "wpg": (0.5 * w_ple_gate).astype(bf),
        "wpp": (0.5 * w_ple_proj).astype(bf),
    }
    prm.update(_s5_params(s5_a_re, s5_a_im, s5_log_dt, s5_b_re, s5_b_im, s5_c_re, s5_c_im, s5_d))

    xt = x
    for i in range(DEPTH):
        if i == 0:
            ug, ap, sb, xt = _mixer_call(xt, i, prm, batch_major_in=True)
        else:
            ug, ap, sb = _mixer_call(xt, i, prm, batch_major_in=False)
        yg = _s5_call(ug, i, prm)
        x1 = _merge_call(xt, yg, ap, sb, i, prm)
        xt = _ffn_call(x1, p, i, prm, final=(i == DEPTH - 1))
    return xt
```

```python
import functools
import math

import jax
import jax.numpy as jnp
from jax import lax
from jax.experimental import pallas as pl
from jax.experimental.pallas import tpu as pltpu

D_MODEL = 1024
BATCH = 8
SEQ = 4096
DEPTH = 2
PLE_DIM = 256
LRU_WIDTH = 1280
LRU_HEADS = 10
LRU_HEAD_DIM = 128
LRU_CONV = 4
LRU_C = 8.0
S5_GROUP = 16
S5_GROUPS = 64
S5_STATE = 64
D_FF = 3 * D_MODEL
FFN_CONV = 3
EPS = 1e-6

LANES = 128
ROWS = SEQ * BATCH
PAIR = 2 * LRU_HEAD_DIM
N_PAIRS = LRU_WIDTH // PAIR
S5_CHUNK = 16
S5_VEC = S5_CHUNK * S5_GROUP
S5_NCHUNK = SEQ // S5_CHUNK
S5_ROWS = S5_NCHUNK * BATCH
S5_REAL = 2 * S5_STATE
CHUNK_ROWS = S5_CHUNK * BATCH
BLOCKS = LANES // S5_GROUP

O_GB = LRU_WIDTH
O_UB = 2 * LRU_WIDTH
O_MA = O_UB + D_MODEL
O_MB = O_MA + D_MODEL

MIXER_ROWS = 512
MERGE_ROWS = 1024
MERGE_SUB = 256
MERGE_COLS = 256
FFN_ROWS = 1024
FFN_COLS = 256
S5_GROUPS_PER_STEP = 4
VMEM_LIMIT_BYTES = 60 * 1024 * 1024

_GELU_C = math.sqrt(2.0 / math.pi)


def _gelu2(x):
    return x * (1.0 + jnp.tanh(x * (_GELU_C + (_GELU_C * 0.044715) * (x * x))))


def _sigmoid2(half_x):
    return jnp.tanh(half_x) + 1.0


def _rmsnorm(x, g):
    ms = jnp.mean(x * x, axis=-1, keepdims=True)
    return x * lax.rsqrt(ms + EPS) * g


def _dot(a, b):
    return jnp.dot(a, b, preferred_element_type=jnp.float32)


def _layer_spec(w, layer):
    zeros = (0,) * (w.ndim - 1)
    return pl.BlockSpec((None,) + w.shape[1:], lambda i: (layer,) + zeros,
                        pipeline_mode=pl.Buffered(1))


def _row_spec(rows, cols):
    return pl.BlockSpec((rows, cols), lambda i: (i, 0))


def _batch_major_spec(rows, cols):
    return pl.BlockSpec((BATCH, rows // BATCH, cols), lambda i: (0, i, 0))


def _to_time_major(src_ref, dst_ref):
    for t in range(src_ref.shape[1]):
        dst_ref[t * BATCH:(t + 1) * BATCH, :] = src_ref[:, t, :]


def _from_time_major(src, dst_ref):
    for t in range(dst_ref.shape[1]):
        dst_ref[:, t, :] = src[t * BATCH:(t + 1) * BATCH, :]


def _group_spec(rows):
    return pl.BlockSpec((S5_GROUPS, rows // S5_CHUNK, S5_VEC), lambda i: (0, i, 0))


def _lane_roll(a, shift):
    if a.dtype == jnp.float32:
        return pltpu.roll(a, shift, axis=1)
    return jnp.concatenate([a[:, LANES - shift:], a[:, :LANES - shift]], axis=1)


def _block_transpose(v):
    blk = lax.broadcasted_iota(jnp.int32, v[0].shape, 1) // S5_GROUP
    for d in (4, 2, 1):
        bit = (blk & d) != 0
        out = list(v)
        for i in range(BLOCKS):
            if i & d == 0:
                lo, hi = v[i], v[i + d]
                out[i] = jnp.where(bit, _lane_roll(hi, S5_GROUP * d), lo)
                out[i + d] = jnp.where(bit, hi, _lane_roll(lo, LANES - S5_GROUP * d))
        v = out
    return v


def _mixer_kernel(x_ref, g_ref, win_ref, cw_ref, cb_ref, gw_ref, gxb_ref, gab_ref, lam_ref,
                  wao_ref, ug_ref, ap_ref, sb_ref, *rest, batch_major_in):
    if batch_major_in:
        xt_ref, ext_s, a_s, u_s, gb_s, ma_s, ub_s, hg_s, hc_s = rest
        _to_time_major(x_ref, xt_ref)
        x_ref = xt_ref
    else:
        ext_s, a_s, u_s, gb_s, ma_s, ub_s, hg_s, hc_s = rest
    rows = x_ref.shape[0]
    nchunk = rows // CHUNK_ROWS
    halo = (LRU_CONV - 1) * BATCH

    @pl.when(pl.program_id(0) == 0)
    def _():
        ext_s[0:halo, :] = jnp.zeros((halo, LRU_WIDTH), jnp.float32)
        hc_s[...] = jnp.zeros_like(hc_s)

    hb = _rmsnorm(x_ref[...], g_ref[...]).astype(jnp.bfloat16)

    lam = lam_ref[...]
    log_coef = (0.5 * LRU_C) * (jnp.minimum(lam, 0.0) - jnp.log1p(jnp.exp(-jnp.abs(lam))))

    for j in range(N_PAIRS):
        c0 = j * PAIR
        xa = _dot(hb, win_ref[:, c0:c0 + PAIR])
        ext_s[halo:halo + rows, c0:c0 + PAIR] = xa
        xc = cb_ref[:, c0:c0 + PAIR] + cw_ref[3:4, c0:c0 + PAIR] * xa
        for k in range(LRU_CONV - 1):
            xc = xc + cw_ref[k:k + 1, c0:c0 + PAIR] * ext_s[k * BATCH:k * BATCH + rows, c0:c0 + PAIR]
        xcb = xc.astype(jnp.bfloat16)
        for hh in range(2):
            lo = hh * LRU_HEAD_DIM
            cs = slice(c0 + lo, c0 + lo + LRU_HEAD_DIM)
            gates = _dot(xcb[:, lo:lo + LRU_HEAD_DIM], gw_ref[2 * j + hh])
            gx2 = _sigmoid2(gates[:, :LRU_HEAD_DIM] + gxb_ref[:, cs])
            log_a = log_coef[:, cs] * jnp.tanh(gates[:, LRU_HEAD_DIM:] + gab_ref[:, cs]) + log_coef[:, cs]
            a = jnp.exp(log_a)
            mult = jnp.sqrt(jnp.maximum((1.0 - a) * (1.0 + a), 1e-12))
            a_s[:, cs] = a
            u_s[:, cs] = mult * (gx2 * xc[:, lo:lo + LRU_HEAD_DIM])
        gb_s[:, c0:c0 + PAIR] = _gelu2(_dot(hb, win_ref[:, O_GB + c0:O_GB + c0 + PAIR]))

    ext_s[0:halo, :] = ext_s[rows:rows + halo, :]

    ma_s[...] = _sigmoid2(_dot(hb, win_ref[:, O_MA:O_MA + D_MODEL]))
    sb_ref[...] = _sigmoid2(_dot(hb, win_ref[:, O_MB:O_MB + D_MODEL])).astype(sb_ref.dtype)

    ub_s[...] = _dot(hb, win_ref[:, O_UB:O_UB + D_MODEL]).reshape(nchunk, S5_CHUNK, BATCH, D_MODEL)
    for tile in range(D_MODEL // LANES):
        for half in range(S5_CHUNK // BLOCKS):
            v = [ub_s[:, half * BLOCKS + s, :, tile * LANES:(tile + 1) * LANES].reshape(nchunk * BATCH, LANES)
                 for s in range(BLOCKS)]
            w = _block_transpose(v)
            for gi in range(BLOCKS):
                ug_ref[tile * BLOCKS + gi, :, half * LANES:(half + 1) * LANES] = w[gi].astype(ug_ref.dtype)

    h = hc_s[...]
    for t in range(rows // BATCH):
        r = t * BATCH
        h = a_s[r:r + BATCH, :] * h + u_s[r:r + BATCH, :]
        hg_s[r:r + BATCH, :] = h * gb_s[r:r + BATCH, :]
    hc_s[...] = h

    ya = _dot(hg_s[...].astype(jnp.bfloat16), wao_ref[...])
    ap_ref[...] = (ma_s[...] * ya).astype(ap_ref.dtype)


def _mixer_call(x, layer, prm, batch_major_in):
    rows = MIXER_ROWS
    halo = (LRU_CONV - 1) * BATCH
    out = jax.ShapeDtypeStruct((ROWS, D_MODEL), jnp.bfloat16)
    grp = jax.ShapeDtypeStruct((S5_GROUPS, S5_ROWS, S5_VEC), jnp.bfloat16)
    weights = [prm[k] for k in ("g_mix", "w_in", "cw", "cb", "gw", "gxb", "gab", "lam", "wao")]
    f32 = jnp.float32
    x_spec = _batch_major_spec(rows, D_MODEL) if batch_major_in else _row_spec(rows, D_MODEL)
    extra = batch_major_in * [jax.ShapeDtypeStruct((ROWS, D_MODEL), f32)]
    return pl.pallas_call(
        functools.partial(_mixer_kernel, batch_major_in=batch_major_in),
        grid=(ROWS // rows,),
        in_specs=[x_spec] + [_layer_spec(w, layer) for w in weights],
        out_specs=[_group_spec(rows)] + [_row_spec(rows, D_MODEL)] * (2 + len(extra)),
        out_shape=[grp, out, out] + extra,
        scratch_shapes=[
            pltpu.VMEM((halo + rows, LRU_WIDTH), f32),
            pltpu.VMEM((rows, LRU_WIDTH), f32),
            pltpu.VMEM((rows, LRU_WIDTH), f32),
            pltpu.VMEM((rows, LRU_WIDTH), f32),
            pltpu.VMEM((rows, D_MODEL), f32),
            pltpu.VMEM((rows // CHUNK_ROWS, S5_CHUNK, BATCH, D_MODEL), f32),
            pltpu.VMEM((rows, LRU_WIDTH), f32),
            pltpu.VMEM((BATCH, LRU_WIDTH), f32),
        ],
        compiler_params=pltpu.CompilerParams(
            dimension_semantics=("arbitrary",), vmem_limit_bytes=VMEM_LIMIT_BYTES),
        name="mixer_in" if batch_major_in else "mixer",
    )(x, *weights)


def _s5_kernel(u_ref, wu_ref, bbt_ref, cp0_ref, cp1_ref, dm_ref, coef_ref, y_ref,
               v_s, vsw_s, xs_s, tw_s):
    ngroups = u_ref.shape[0]
    hi = lax.Precision.HIGHEST
    lane = lax.broadcasted_iota(jnp.int32, (S5_GROUP, S5_VEC), 1)
    for g in range(ngroups):
        taps = (jnp.dot(bbt_ref[g, 0], cp0_ref[g, 0], precision=hi, preferred_element_type=jnp.float32)
                - jnp.dot(bbt_ref[g, 1], cp0_ref[g, 1], precision=hi, preferred_element_type=jnp.float32)
                + dm_ref[g])
        tw_s[g, 0:S5_GROUP, :] = taps.astype(jnp.bfloat16)
        for s in range(1, S5_CHUNK):
            shifted = jnp.where(lane >= s * S5_GROUP, pltpu.roll(taps, s * S5_GROUP, axis=1), 0.0)
            tw_s[g, s * S5_GROUP:(s + 1) * S5_GROUP, :] = shifted.astype(jnp.bfloat16)
        tw_s[g, S5_VEC:S5_VEC + S5_STATE, :] = cp1_ref[g, 0].astype(jnp.bfloat16)
        tw_s[g, S5_VEC + S5_STATE:, :] = (-cp1_ref[g, 1]).astype(jnp.bfloat16)

        v = _dot(u_ref[g], wu_ref[g])
        v_s[g] = v
        vsw_s[g] = pltpu.roll(v, S5_STATE, axis=1)

    zero = jnp.zeros((BATCH, S5_REAL), jnp.float32)
    state = [(zero, zero)] * ngroups
    for k in range(S5_NCHUNK):
        rs = slice(k * BATCH, (k + 1) * BATCH)
        for g in range(ngroups):
            x, xsw = state[g]
            xs_s[g, rs, :] = x
            ar = coef_ref[g, 0]
            ai = coef_ref[g, 1]
            state[g] = (ar * x + ai * xsw + v_s[g, rs, :], ar * xsw - ai * x + vsw_s[g, rs, :])

    for g in range(ngroups):
        y = _dot(u_ref[g], tw_s[g, 0:S5_VEC, :])
        y = y + _dot(xs_s[g].astype(jnp.bfloat16), tw_s[g, S5_VEC:, :])
        y_ref[g] = y.astype(y_ref.dtype)


def _s5_call(u_grp, layer, prm):
    gp = S5_GROUPS_PER_STEP
    weights = [prm[k] for k in ("s5_wu", "s5_bbt", "s5_cp0", "s5_cp1", "s5_dm", "s5_coef")]

    def wspec(w):
        zeros = (0,) * (w.ndim - 2)
        return pl.BlockSpec((None, gp) + w.shape[2:], lambda i: (layer, i) + zeros)

    blk = pl.BlockSpec((gp, S5_ROWS, S5_VEC), lambda i: (i, 0, 0))
    return pl.pallas_call(
        _s5_kernel,
        grid=(S5_GROUPS // gp,),
        in_specs=[blk] + [wspec(w) for w in weights],
        out_specs=blk,
        out_shape=jax.ShapeDtypeStruct((S5_GROUPS, S5_ROWS, S5_VEC), jnp.bfloat16),
        scratch_shapes=[pltpu.VMEM((gp, S5_ROWS, S5_REAL), jnp.float32)] * 3
        + [pltpu.VMEM((gp, S5_VEC + S5_REAL, S5_VEC), jnp.bfloat16)],
        compiler_params=pltpu.CompilerParams(
            dimension_semantics=("arbitrary",), vmem_limit_bytes=VMEM_LIMIT_BYTES),
        name="s5",
    )(u_grp, *weights)


def _merge_kernel(x_ref, yg_ref, ap_ref, sb_ref, wglu_ref, bglu_ref, wo_ref, o_ref, ys_s):
    nchunk = MERGE_SUB // CHUNK_ROWS
    assert nchunk * BATCH == 16
    grows = MERGE_SUB // S5_CHUNK

    def step_major(a):
        a = a.reshape(nchunk, S5_CHUNK, BATCH, D_MODEL)
        return jnp.swapaxes(a, 0, 1).reshape(MERGE_SUB, D_MODEL)

    def chunk_major(a):
        a = a.reshape(S5_CHUNK, nchunk, BATCH, D_MODEL)
        return jnp.swapaxes(a, 0, 1).reshape(MERGE_SUB, D_MODEL)

    for sub in range(x_ref.shape[0] // MERGE_SUB):
        rs = slice(sub * MERGE_SUB, (sub + 1) * MERGE_SUB)
        gs = slice(sub * grows, (sub + 1) * grows)
        for tile in range(D_MODEL // LANES):
            ls = slice(tile * LANES, (tile + 1) * LANES)
            for half in range(S5_CHUNK // BLOCKS):
                w = [yg_ref[tile * BLOCKS + gi, gs, half * LANES:(half + 1) * LANES] for gi in range(BLOCKS)]
                v = _block_transpose(w)
                for s in range(BLOCKS):
                    step = half * BLOCKS + s
                    ys_s[sub, step * nchunk * BATCH:(step + 1) * nchunk * BATCH, ls] = (
                        _gelu2(v[s].astype(jnp.float32)).astype(jnp.bfloat16))

        ys = ys_s[sub]
        ap = step_major(ap_ref[rs, :].astype(jnp.float32))
        sb = step_major(sb_ref[rs, :].astype(jnp.float32))
        blocks = []
        for c0 in range(0, D_MODEL, MERGE_COLS):
            ca = slice(c0, c0 + MERGE_COLS)
            cb = slice(D_MODEL + c0, D_MODEL + c0 + MERGE_COLS)
            glu_a = _dot(ys, wglu_ref[:, ca]) + bglu_ref[:, ca]
            glu_b = _dot(ys, wglu_ref[:, cb]) + bglu_ref[:, cb]
            blocks.append((ap[:, ca] + sb[:, ca] * (glu_a * _sigmoid2(glu_b))).astype(jnp.bfloat16))
        merged = jnp.concatenate(blocks, axis=1)
        o_ref[rs, :] = x_ref[rs, :] + chunk_major(_dot(merged, wo_ref[...]))


def _merge_call(xt, yg, ap, sb, layer, prm):
    rows = MERGE_ROWS
    weights = [prm[k] for k in ("wglu", "bglu", "wo")]
    return pl.pallas_call(
        _merge_kernel,
        grid=(ROWS // rows,),
        in_specs=[_row_spec(rows, D_MODEL), _group_spec(rows), _row_spec(rows, D_MODEL),
                  _row_spec(rows, D_MODEL)] + [_layer_spec(w, layer) for w in weights],
        out_specs=_row_spec(rows, D_MODEL),
        out_shape=jax.ShapeDtypeStruct((ROWS, D_MODEL), jnp.float32),
        scratch_shapes=[pltpu.VMEM((rows // MERGE_SUB, MERGE_SUB, D_MODEL), jnp.bfloat16)],
        compiler_params=pltpu.CompilerParams(
            dimension_semantics=("arbitrary",), vmem_limit_bytes=VMEM_LIMIT_BYTES),
        name="merge",
    )(xt, yg, ap, sb, *weights)


def _ffn_kernel(x_ref, p_ref, gffn_ref, wup_ref, cfw_ref, cfb_ref, wdn_ref,
                gple_ref, wpg_ref, wpp_ref, gfin_ref,
                o_ref, act_s, fc_s, pt_s, *, final):
    rows = x_ref.shape[0]
    halo = (FFN_CONV - 1) * BATCH

    @pl.when(pl.program_id(0) == 0)
    def _():
        fc_s[...] = jnp.zeros_like(fc_s)

    x1 = x_ref[...]
    h2 = _rmsnorm(x1, gffn_ref[...]).astype(jnp.bfloat16)

    def conv_block(col):
        up = _dot(h2, wup_ref[:, col:col + FFN_COLS])
        ext = jnp.concatenate([fc_s[:, col:col + FFN_COLS], up], axis=0)
        fc_s[:, col:col + FFN_COLS] = up[rows - halo:rows, :]
        out = cfb_ref[:, col:col + FFN_COLS] + cfw_ref[2:3, col:col + FFN_COLS] * up
        for k in range(FFN_CONV - 1):
            out = out + cfw_ref[k:k + 1, col:col + FFN_COLS] * ext[k * BATCH:k * BATCH + rows, :]
        return out

    for c in range(D_FF // FFN_COLS):
        col = c * FFN_COLS
        act = _gelu2(conv_block(col)) * conv_block(D_FF + col)
        act_s[:, col:col + FFN_COLS] = act.astype(jnp.bfloat16)

    x2 = x1 + _dot(act_s[...], wdn_ref[...])

    h3 = _rmsnorm(x2, gple_ref[...]).astype(jnp.bfloat16)
    gate = _sigmoid2(_dot(h3, wpg_ref[...]))
    _to_time_major(p_ref, pt_s)
    x3 = x2 + gate * _dot(pt_s[...].astype(jnp.bfloat16), wpp_ref[...])
    if final:
        x3 = _rmsnorm(x3, gfin_ref[...])
        _from_time_major(x3, o_ref)
    else:
        o_ref[...] = x3


def _ffn_call(x1, p, layer, prm, final):
    rows = FFN_ROWS
    halo = (FFN_CONV - 1) * BATCH
    weights = [prm[k] for k in ("g_ffn", "wup", "cfw", "cfb", "wdn", "g_ple", "wpg", "wpp")]
    return pl.pallas_call(
        functools.partial(_ffn_kernel, final=final),
        grid=(ROWS // rows,),
        in_specs=[_row_spec(rows, D_MODEL),
                  pl.BlockSpec((None, BATCH, rows // BATCH, PLE_DIM), lambda i: (layer, 0, i, 0))]
        + [_layer_spec(w, layer) for w in weights] + [_layer_spec(prm["g_final"], 0)],
        out_specs=_batch_major_spec(rows, D_MODEL) if final else _row_spec(rows, D_MODEL),
        out_shape=jax.ShapeDtypeStruct((BATCH, SEQ, D_MODEL) if final else (ROWS, D_MODEL),
                                       jnp.float32),
        scratch_shapes=[
            pltpu.VMEM((rows, D_FF), jnp.bfloat16),
            pltpu.VMEM((halo, 2 * D_FF), jnp.float32),
            pltpu.VMEM((rows, PLE_DIM), jnp.float32),
        ],
        compiler_params=pltpu.CompilerParams(
            dimension_semantics=("arbitrary",), vmem_limit_bytes=VMEM_LIMIT_BYTES),
        name="ffn_final" if final else "ffn",
    )(x1, p, *weights, prm["g_final"])


def _s5_params(a_re, a_im, log_dt, b_re, b_im, c_re, c_im, d):
    dt = jnp.exp(log_dt)[..., None]
    lr = a_re * dt
    li = a_im * dt
    mag = jnp.exp(lr)
    lb_re = mag * jnp.cos(li)
    lb_im = mag * jnp.sin(li)
    nr = lb_re - 1.0
    ni = lb_im
    den = a_re * a_re + a_im * a_im
    coef_re = (nr * a_re + ni * a_im) / den
    coef_im = (ni * a_re - nr * a_im) / den
    bb_re = coef_re[..., None] * b_re - coef_im[..., None] * b_im
    bb_im = coef_re[..., None] * b_im + coef_im[..., None] * b_re

    n = jnp.arange(S5_CHUNK + 1, dtype=jnp.float32)
    pmag = jnp.exp(lr[..., None] * n)
    pw_re = pmag * jnp.cos(li[..., None] * n)
    pw_im = pmag * jnp.sin(li[..., None] * n)

    ct_re = jnp.tile(jnp.swapaxes(c_re, -1, -2), (1, 1, 1, S5_CHUNK + 1))
    ct_im = jnp.tile(jnp.swapaxes(c_im, -1, -2), (1, 1, 1, S5_CHUNK + 1))
    pr_re = jnp.repeat(pw_re, S5_GROUP, axis=-1)
    pr_im = jnp.repeat(pw_im, S5_GROUP, axis=-1)
    cp_re = ct_re * pr_re - ct_im * pr_im
    cp_im = ct_re * pr_im + ct_im * pr_re
    lead = cp_re.shape[:3]
    cp0 = jnp.stack([cp_re[..., :S5_VEC], cp_im[..., :S5_VEC]], axis=2)
    cp1 = jnp.stack([cp_re[..., S5_GROUP:], cp_im[..., S5_GROUP:]], axis=2)

    bbt_re = jnp.swapaxes(bb_re, -1, -2)
    bbt_im = jnp.swapaxes(bb_im, -1, -2)
    bbt = jnp.stack([bbt_re, bbt_im], axis=2)

    rev_re = jnp.swapaxes(pw_re[..., S5_CHUNK - 1::-1], -1, -2)[..., None, :]
    rev_im = jnp.swapaxes(pw_im[..., S5_CHUNK - 1::-1], -1, -2)[..., None, :]
    wu_re = rev_re * bbt_re[..., None, :, :] - rev_im * bbt_im[..., None, :, :]
    wu_im = rev_re * bbt_im[..., None, :, :] + rev_im * bbt_re[..., None, :, :]
    wu = jnp.concatenate([wu_re, wu_im], axis=-1).reshape(lead[:2] + (S5_VEC, S5_REAL))

    dm = jnp.eye(S5_GROUP, dtype=jnp.float32) * d.reshape(lead[:2] + (1, S5_GROUP))
    dm = jnp.pad(dm, ((0, 0), (0, 0), (0, 0), (0, S5_VEC - S5_GROUP)))

    l_re, l_im = pw_re[..., S5_CHUNK], pw_im[..., S5_CHUNK]
    ar = jnp.concatenate([l_re, l_re], axis=-1)
    ai = jnp.concatenate([-l_im, l_im], axis=-1)
    coef = jnp.broadcast_to(jnp.stack([ar, ai], axis=2)[..., None, :],
                            lead[:2] + (2, BATCH, S5_REAL))
    return {"s5_wu": wu.astype(jnp.bfloat16), "s5_bbt": bbt, "s5_cp0": cp0, "s5_cp1": cp1,
            "s5_dm": dm, "s5_coef": coef}


def kernel(x, p, g_mix, w_in, conv_a_w, conv_a_b, gate_x_w, gate_x_b, gate_a_w, gate_a_b, lru_lambda, w_a_out, s5_a_re, s5_a_im, s5_log_dt, s5_b_re, s5_b_im, s5_c_re, s5_c_im, s5_d, w_glu, b_glu, w_o, g_ffn, w_up, conv_f_w, conv_f_b, w_down, g_ple, w_ple_gate, w_ple_proj, g_final):
    assert x.shape == (BATCH, SEQ, D_MODEL) and p.shape == (DEPTH, BATCH, SEQ, PLE_DIM)
    bf = jnp.bfloat16
    row = lambda v: v[:, None, :]
    ones = functools.partial(jnp.full, dtype=jnp.float32)
    in_scale = jnp.concatenate([ones((O_MA,), 1.0), ones((2 * D_MODEL,), 0.5)])
    glu_scale = jnp.concatenate([ones((D_MODEL,), 0.125), ones((D_MODEL,), 0.25)])
    prm = {
        "g_mix": row(g_mix), "g_ffn": row(g_ffn), "g_ple": row(g_ple), "g_final": g_final[None, None],
        "w_in": (w_in * in_scale).astype(bf), "cw": conv_a_w, "cb": row(conv_a_b),
        "gw": (0.5 * jnp.concatenate([gate_x_w, gate_a_w], axis=-1)).astype(bf),
        "gxb": row(0.5 * gate_x_b), "gab": row(0.5 * gate_a_b), "lam": row(lru_lambda),
        "wao": (0.125 * w_a_out).astype(bf),
        "wglu": (w_glu * glu_scale).astype(bf), "bglu": row(b_glu * (2.0 * glu_scale)),
        "wo": w_o.astype(bf),
        "wup": w_up.astype(bf), "cfw": conv_f_w, "cfb": row(conv_f_b),
        "wdn": (0.5 * w_down).astype(bf), "wpg": (0.5 * w_ple_gate).astype(bf),
        "wpp": (0.5 * w_ple_proj).astype(bf),
    }
    prm.update(_s5_params(s5_a_re, s5_a_im, s5_log_dt, s5_b_re, s5_b_im, s5_c_re, s5_c_im, s5_d))

    xt = x
    for i in range(DEPTH):
        if i == 0:
            ug, ap, sb, xt = _mixer_call(xt, i, prm, batch_major_in=True)
        else:
            ug, ap, sb = _mixer_call(xt, i, prm, batch_major_in=False)
        yg = _s5_call(ug, i, prm)
        x1 = _merge_call(xt, yg, ap, sb, i, prm)
        xt = _ffn_call(x1, p, i, prm, final=(i == DEPTH - 1))
    return xt
```

```python
import functools
import math

import jax
import jax.numpy as jnp
from jax import lax
from jax.experimental import pallas as pl
from jax.experimental.pallas import tpu as pltpu

D_MODEL = 1024
BATCH = 8
SEQ = 4096
DEPTH = 2
PLE_DIM = 256
LRU_WIDTH = 1280
LRU_HEADS = 10
LRU_HEAD_DIM = 128
LRU_CONV = 4
LRU_C = 8.0
S5_GROUP = 16
S5_GROUPS = 64
S5_STATE = 64
D_FF = 3 * D_MODEL
FFN_CONV = 3
EPS = 1e-6

LANES = 128
ROWS = SEQ * BATCH
PAIR = 2 * LRU_HEAD_DIM
N_PAIRS = LRU_WIDTH // PAIR
S5_CHUNK = 16
S5_VEC = S5_CHUNK * S5_GROUP
S5_NCHUNK = SEQ // S5_CHUNK
S5_ROWS = S5_NCHUNK * BATCH
S5_REAL = 2 * S5_STATE
CHUNK_ROWS = S5_CHUNK * BATCH
BLOCKS = LANES // S5_GROUP

O_GB = LRU_WIDTH
O_UB = 2 * LRU_WIDTH
O_MA = O_UB + D_MODEL
O_MB = O_MA + D_MODEL

MIXER_ROWS = 512
MERGE_ROWS = 1024
MERGE_SUB = 256
MERGE_COLS = 256
FFN_ROWS = 1024
FFN_COLS = 256
S5_GROUPS_PER_STEP = 4
VMEM_LIMIT_BYTES = 60 * 1024 * 1024

_GELU_C = math.sqrt(2.0 / math.pi)


def _gelu2(x):
    return x * (1.0 + jnp.tanh(x * (_GELU_C + (_GELU_C * 0.044715) * (x * x))))


def _sigmoid2(half_x):
    return jnp.tanh(half_x) + 1.0


def _rmsnorm(x, g):
    ms = jnp.mean(x * x, axis=-1, keepdims=True)
    return x * lax.rsqrt(ms + EPS) * g


def _dot(a, b):
    return jnp.dot(a, b, preferred_element_type=jnp.float32)


def _layer_spec(w, layer):
    zeros = (0,) * (w.ndim - 1)
    return pl.BlockSpec((None,) + w.shape[1:], lambda i: (layer,) + zeros,
                        pipeline_mode=pl.Buffered(1))


def _row_spec(rows, cols):
    return pl.BlockSpec((rows, cols), lambda i: (i, 0))


def _batch_major_spec(rows, cols):
    return pl.BlockSpec((BATCH, rows // BATCH, cols), lambda i: (0, i, 0))


def _to_time_major(src_ref, dst_ref):
    for t in range(src_ref.shape[1]):
        dst_ref[t * BATCH:(t + 1) * BATCH, :] = src_ref[:, t, :]


def _from_time_major(src, dst_ref):
    for t in range(dst_ref.shape[1]):
        dst_ref[:, t, :] = src[t * BATCH:(t + 1) * BATCH, :]


def _group_spec(rows):
    return pl.BlockSpec((S5_GROUPS, rows // S5_CHUNK, S5_VEC), lambda i: (0, i, 0))


def _lane_roll(a, shift):
    if a.dtype == jnp.float32:
        return pltpu.roll(a, shift, axis=1)
    return jnp.concatenate([a[:, LANES - shift:], a[:, :LANES - shift]], axis=1)


def _block_transpose(v):
    blk = lax.broadcasted_iota(jnp.int32, v[0].shape, 1) // S5_GROUP
    for d in (4, 2, 1):
        bit = (blk & d) != 0
        out = list(v)
        for i in range(BLOCKS):
            if i & d == 0:
                lo, hi = v[i], v[i + d]
                out[i] = jnp.where(bit, _lane_roll(hi, S5_GROUP * d), lo)
                out[i + d] = jnp.where(bit, hi, _lane_roll(lo, LANES - S5_GROUP * d))
        v = out
    return v


def _mixer_kernel(x_ref, g_ref, win_ref, cw_ref, cb_ref, gw_ref, gxb_ref, gab_ref, lam_ref,
                  wao_ref, ug_ref, ap_ref, sb_ref, *rest, batch_major_in):
    if batch_major_in:
        xt_ref, ext_s, a_s, u_s, gb_s, ma_s, ub_s, hg_s, hc_s = rest
        _to_time_major(x_ref, xt_ref)
        x_ref = xt_ref
    else:
        ext_s, a_s, u_s, gb_s, ma_s, ub_s, hg_s, hc_s = rest
    rows = x_ref.shape[0]
    nchunk = rows // CHUNK_ROWS
    halo = (LRU_CONV - 1) * BATCH

    @pl.when(pl.program_id(0) == 0)
    def _():
        ext_s[0:halo, :] = jnp.zeros((halo, LRU_WIDTH), jnp.float32)
        hc_s[...] = jnp.zeros_like(hc_s)

    hb = _rmsnorm(x_ref[...], g_ref[...]).astype(jnp.bfloat16)

    lam = lam_ref[...]
    log_coef = (0.5 * LRU_C) * (jnp.minimum(lam, 0.0) - jnp.log1p(jnp.exp(-jnp.abs(lam))))

    for j in range(N_PAIRS):
        c0 = j * PAIR
        xa = _dot(hb, win_ref[:, c0:c0 + PAIR])
        ext_s[halo:halo + rows, c0:c0 + PAIR] = xa
        xc = cb_ref[:, c0:c0 + PAIR] + cw_ref[3:4, c0:c0 + PAIR] * xa
        for k in range(LRU_CONV - 1):
            xc = xc + cw_ref[k:k + 1, c0:c0 + PAIR] * ext_s[k * BATCH:k * BATCH + rows, c0:c0 + PAIR]
        xcb = xc.astype(jnp.bfloat16)
        for hh in range(2):
            lo = hh * LRU_HEAD_DIM
            cs = slice(c0 + lo, c0 + lo + LRU_HEAD_DIM)
            gates = _dot(xcb[:, lo:lo + LRU_HEAD_DIM], gw_ref[2 * j + hh])
            gx2 = _sigmoid2(gates[:, :LRU_HEAD_DIM] + gxb_ref[:, cs])
            log_a = log_coef[:, cs] * jnp.tanh(gates[:, LRU_HEAD_DIM:] + gab_ref[:, cs]) + log_coef[:, cs]
            a = jnp.exp(log_a)
            mult = jnp.sqrt(jnp.maximum((1.0 - a) * (1.0 + a), 1e-12))
            a_s[:, cs] = a
            u_s[:, cs] = mult * (gx2 * xc[:, lo:lo + LRU_HEAD_DIM])
        gb_s[:, c0:c0 + PAIR] = _gelu2(_dot(hb, win_ref[:, O_GB + c0:O_GB + c0 + PAIR]))

    ext_s[0:halo, :] = ext_s[rows:rows + halo, :]

    ma_s[...] = _sigmoid2(_dot(hb, win_ref[:, O_MA:O_MA + D_MODEL]))
    sb_ref[...] = _sigmoid2(_dot(hb, win_ref[:, O_MB:O_MB + D_MODEL])).astype(sb_ref.dtype)

    ub_s[...] = _dot(hb, win_ref[:, O_UB:O_UB + D_MODEL]).reshape(nchunk, S5_CHUNK, BATCH, D_MODEL)
    for tile in range(D_MODEL // LANES):
        for half in range(S5_CHUNK // BLOCKS):
            v = [ub_s[:, half * BLOCKS + s, :, tile * LANES:(tile + 1) * LANES].reshape(nchunk * BATCH, LANES)
                 for s in range(BLOCKS)]
            w = _block_transpose(v)
            for gi in range(BLOCKS):
                ug_ref[tile * BLOCKS + gi, :, half * LANES:(half + 1) * LANES] = w[gi].astype(ug_ref.dtype)

    h = hc_s[...]
    for t in range(rows // BATCH):
        r = t * BATCH
        h = a_s[r:r + BATCH, :] * h + u_s[r:r + BATCH, :]
        hg_s[r:r + BATCH, :] = h * gb_s[r:r + BATCH, :]
    hc_s[...] = h

    hg = hg_s[...].astype(jnp.bfloat16)
    for c0 in range(0, D_MODEL, PAIR):
        ya = _dot(hg, wao_ref[:, c0:c0 + PAIR])
        ap_ref[:, c0:c0 + PAIR] = (ma_s[:, c0:c0 + PAIR] * ya).astype(ap_ref.dtype)


def _mixer_call(x, layer, prm, batch_major_in):
    rows = MIXER_ROWS
    halo = (LRU_CONV - 1) * BATCH
    out = jax.ShapeDtypeStruct((ROWS, D_MODEL), jnp.bfloat16)
    grp = jax.ShapeDtypeStruct((S5_GROUPS, S5_ROWS, S5_VEC), jnp.bfloat16)
    weights = [prm[k] for k in ("g_mix", "w_in", "cw", "cb", "gw", "gxb", "gab", "lam", "wao")]
    f32 = jnp.float32
    x_spec = _batch_major_spec(rows, D_MODEL) if batch_major_in else _row_spec(rows, D_MODEL)
    extra = batch_major_in * [jax.ShapeDtypeStruct((ROWS, D_MODEL), f32)]
    return pl.pallas_call(
        functools.partial(_mixer_kernel, batch_major_in=batch_major_in),
        grid=(ROWS // rows,),
        in_specs=[x_spec] + [_layer_spec(w, layer) for w in weights],
        out_specs=[_group_spec(rows)] + [_row_spec(rows, D_MODEL)] * (2 + len(extra)),
        out_shape=[grp, out, out] + extra,
        scratch_shapes=[
            pltpu.VMEM((halo + rows, LRU_WIDTH), f32),
            pltpu.VMEM((rows, LRU_WIDTH), f32),
            pltpu.VMEM((rows, LRU_WIDTH), f32),
            pltpu.VMEM((rows, LRU_WIDTH), f32),
            pltpu.VMEM((rows, D_MODEL), f32),
            pltpu.VMEM((rows // CHUNK_ROWS, S5_CHUNK, BATCH, D_MODEL), f32),
            pltpu.VMEM((rows, LRU_WIDTH), f32),
            pltpu.VMEM((BATCH, LRU_WIDTH), f32),
        ],
        compiler_params=pltpu.CompilerParams(
            dimension_semantics=("arbitrary",), vmem_limit_bytes=VMEM_LIMIT_BYTES),
        name="mixer_in" if batch_major_in else "mixer",
    )(x, *weights)


def _s5_kernel(u_ref, wu_ref, bbt_ref, cp0_ref, cp1_ref, dm_ref, coef_ref, y_ref,
               v_s, vsw_s, xs_s, tw_s):
    ngroups = u_ref.shape[0]
    hi = lax.Precision.HIGHEST
    lane = lax.broadcasted_iota(jnp.int32, (S5_GROUP, S5_VEC), 1)
    for g in range(ngroups):
        taps = (jnp.dot(bbt_ref[g, 0], cp0_ref[g, 0], precision=hi, preferred_element_type=jnp.float32)
                - jnp.dot(bbt_ref[g, 1], cp0_ref[g, 1], precision=hi, preferred_element_type=jnp.float32)
                + dm_ref[g])
        tw_s[g, 0:S5_GROUP, :] = taps.astype(jnp.bfloat16)
        for s in range(1, S5_CHUNK):
            shifted = jnp.where(lane >= s * S5_GROUP, pltpu.roll(taps, s * S5_GROUP, axis=1), 0.0)
            tw_s[g, s * S5_GROUP:(s + 1) * S5_GROUP, :] = shifted.astype(jnp.bfloat16)
        tw_s[g, S5_VEC:S5_VEC + S5_STATE, :] = cp1_ref[g, 0].astype(jnp.bfloat16)
        tw_s[g, S5_VEC + S5_STATE:, :] = (-cp1_ref[g, 1]).astype(jnp.bfloat16)

        v = _dot(u_ref[g], wu_ref[g])
        v_s[g] = v
        vsw_s[g] = pltpu.roll(v, S5_STATE, axis=1)

    zero = jnp.zeros((BATCH, S5_REAL), jnp.float32)
    state = [(zero, zero)] * ngroups
    for k in range(S5_NCHUNK):
        rs = slice(k * BATCH, (k + 1) * BATCH)
        for g in range(ngroups):
            x, xsw = state[g]
            xs_s[g, rs, :] = x
            ar = coef_ref[g, 0]
            ai = coef_ref[g, 1]
            state[g] = (ar * x + ai * xsw + v_s[g, rs, :], ar * xsw - ai * x + vsw_s[g, rs, :])

    for g in range(ngroups):
        y = _dot(u_ref[g], tw_s[g, 0:S5_VEC, :])
        y = y + _dot(xs_s[g].astype(jnp.bfloat16), tw_s[g, S5_VEC:, :])
        y_ref[g] = y.astype(y_ref.dtype)


def _s5_call(u_grp, layer, prm):
    gp = S5_GROUPS_PER_STEP
    weights = [prm[k] for k in ("s5_wu", "s5_bbt", "s5_cp0", "s5_cp1", "s5_dm", "s5_coef")]

    def wspec(w):
        zeros = (0,) * (w.ndim - 2)
        return pl.BlockSpec((None, gp) + w.shape[2:], lambda i: (layer, i) + zeros)

    blk = pl.BlockSpec((gp, S5_ROWS, S5_VEC), lambda i: (i, 0, 0))
    return pl.pallas_call(
        _s5_kernel,
        grid=(S5_GROUPS // gp,),
        in_specs=[blk] + [wspec(w) for w in weights],
        out_specs=blk,
        out_shape=jax.ShapeDtypeStruct((S5_GROUPS, S5_ROWS, S5_VEC), jnp.bfloat16),
        scratch_shapes=[pltpu.VMEM((gp, S5_ROWS, S5_REAL), jnp.float32)] * 3
        + [pltpu.VMEM((gp, S5_VEC + S5_REAL, S5_VEC), jnp.bfloat16)],
        compiler_params=pltpu.CompilerParams(
            dimension_semantics=("arbitrary",), vmem_limit_bytes=VMEM_LIMIT_BYTES),
        name="s5",
    )(u_grp, *weights)


def _merge_kernel(x_ref, yg_ref, ap_ref, sb_ref, wglu_ref, bglu_ref, wo_ref, o_ref, ys_s):
    nchunk = MERGE_SUB // CHUNK_ROWS
    assert nchunk * BATCH == 16
    grows = MERGE_SUB // S5_CHUNK

    def step_major(a):
        a = a.reshape(nchunk, S5_CHUNK, BATCH, D_MODEL)
        return jnp.swapaxes(a, 0, 1).reshape(MERGE_SUB, D_MODEL)

    def chunk_major(a):
        a = a.reshape(S5_CHUNK, nchunk, BATCH, D_MODEL)
        return jnp.swapaxes(a, 0, 1).reshape(MERGE_SUB, D_MODEL)

    for sub in range(x_ref.shape[0] // MERGE_SUB):
        rs = slice(sub * MERGE_SUB, (sub + 1) * MERGE_SUB)
        gs = slice(sub * grows, (sub + 1) * grows)
        for tile in range(D_MODEL // LANES):
            ls = slice(tile * LANES, (tile + 1) * LANES)
            for half in range(S5_CHUNK // BLOCKS):
                w = [yg_ref[tile * BLOCKS + gi, gs, half * LANES:(half + 1) * LANES] for gi in range(BLOCKS)]
                v = _block_transpose(w)
                for s in range(BLOCKS):
                    step = half * BLOCKS + s
                    ys_s[sub, step * nchunk * BATCH:(step + 1) * nchunk * BATCH, ls] = (
                        _gelu2(v[s].astype(jnp.float32)).astype(jnp.bfloat16))

        ys = ys_s[sub]
        ap = step_major(ap_ref[rs, :].astype(jnp.float32))
        sb = step_major(sb_ref[rs, :].astype(jnp.float32))
        blocks = []
        for c0 in range(0, D_MODEL, MERGE_COLS):
            ca = slice(c0, c0 + MERGE_COLS)
            cb = slice(D_MODEL + c0, D_MODEL + c0 + MERGE_COLS)
            glu_a = _dot(ys, wglu_ref[:, ca]) + bglu_ref[:, ca]
            glu_b = _dot(ys, wglu_ref[:, cb]) + bglu_ref[:, cb]
            blocks.append((ap[:, ca] + sb[:, ca] * (glu_a * _sigmoid2(glu_b))).astype(jnp.bfloat16))
        merged = jnp.concatenate(blocks, axis=1)
        o_ref[rs, :] = x_ref[rs, :] + chunk_major(_dot(merged, wo_ref[...]))


def _merge_call(xt, yg, ap, sb, layer, prm):
    rows = MERGE_ROWS
    weights = [prm[k] for k in ("wglu", "bglu", "wo")]
    return pl.pallas_call(
        _merge_kernel,
        grid=(ROWS // rows,),
        in_specs=[_row_spec(rows, D_MODEL), _group_spec(rows), _row_spec(rows, D_MODEL),
                  _row_spec(rows, D_MODEL)] + [_layer_spec(w, layer) for w in weights],
        out_specs=_row_spec(rows, D_MODEL),
        out_shape=jax.ShapeDtypeStruct((ROWS, D_MODEL), jnp.float32),
        scratch_shapes=[pltpu.VMEM((rows // MERGE_SUB, MERGE_SUB, D_MODEL), jnp.bfloat16)],
        compiler_params=pltpu.CompilerParams(
            dimension_semantics=("arbitrary",), vmem_limit_bytes=VMEM_LIMIT_BYTES),
        name="merge",
    )(xt, yg, ap, sb, *weights)


def _ffn_kernel(x_ref, p_ref, gffn_ref, wup_ref, cfw_ref, cfb_ref, wdn_ref,
                gple_ref, wpg_ref, wpp_ref, gfin_ref,
                o_ref, act_s, fc_s, pt_s, *, final):
    rows = x_ref.shape[0]
    halo = (FFN_CONV - 1) * BATCH

    @pl.when(pl.program_id(0) == 0)
    def _():
        fc_s[...] = jnp.zeros_like(fc_s)

    x1 = x_ref[...]
    h2 = _rmsnorm(x1, gffn_ref[...]).astype(jnp.bfloat16)

    def conv_block(col):
        up = _dot(h2, wup_ref[:, col:col + FFN_COLS])
        ext = jnp.concatenate([fc_s[:, col:col + FFN_COLS], up], axis=0)
        fc_s[:, col:col + FFN_COLS] = up[rows - halo:rows, :]
        out = cfb_ref[:, col:col + FFN_COLS] + cfw_ref[2:3, col:col + FFN_COLS] * up
        for k in range(FFN_CONV - 1):
            out = out + cfw_ref[k:k + 1, col:col + FFN_COLS] * ext[k * BATCH:k * BATCH + rows, :]
        return out

    for c in range(D_FF // FFN_COLS):
        col = c * FFN_COLS
        act = _gelu2(conv_block(col)) * conv_block(D_FF + col)
        act_s[:, col:col + FFN_COLS] = act.astype(jnp.bfloat16)

    x2 = x1 + _dot(act_s[...], wdn_ref[...])

    h3 = _rmsnorm(x2, gple_ref[...]).astype(jnp.bfloat16)
    gate = _sigmoid2(_dot(h3, wpg_ref[...]))
    _to_time_major(p_ref, pt_s)
    x3 = x2 + gate * _dot(pt_s[...].astype(jnp.bfloat16), wpp_ref[...])
    if final:
        x3 = _rmsnorm(x3, gfin_ref[...])
        _from_time_major(x3, o_ref)
    else:
        o_ref[...] = x3


def _ffn_call(x1, p, layer, prm, final):
    rows = FFN_ROWS
    halo = (FFN_CONV - 1) * BATCH
    weights = [prm[k] for k in ("g_ffn", "wup", "cfw", "cfb", "wdn", "g_ple", "wpg", "wpp")]
    return pl.pallas_call(
        functools.partial(_ffn_kernel, final=final),
        grid=(ROWS // rows,),
        in_specs=[_row_spec(rows, D_MODEL),
                  pl.BlockSpec((None, BATCH, rows // BATCH, PLE_DIM), lambda i: (layer, 0, i, 0))]
        + [_layer_spec(w, layer) for w in weights] + [_layer_spec(prm["g_final"], 0)],
        out_specs=_batch_major_spec(rows, D_MODEL) if final else _row_spec(rows, D_MODEL),
        out_shape=jax.ShapeDtypeStruct((BATCH, SEQ, D_MODEL) if final else (ROWS, D_MODEL),
                                       jnp.float32),
        scratch_shapes=[
            pltpu.VMEM((rows, D_FF), jnp.bfloat16),
            pltpu.VMEM((halo, 2 * D_FF), jnp.float32),
            pltpu.VMEM((rows, PLE_DIM), jnp.float32),
        ],
        compiler_params=pltpu.CompilerParams(
            dimension_semantics=("arbitrary",), vmem_limit_bytes=VMEM_LIMIT_BYTES),
        name="ffn_final" if final else "ffn",
    )(x1, p, *weights, prm["g_final"])


def _s5_params(a_re, a_im, log_dt, b_re, b_im, c_re, c_im, d):
    dt = jnp.exp(log_dt)[..., None]
    lr = a_re * dt
    li = a_im * dt
    mag = jnp.exp(lr)
    lb_re = mag * jnp.cos(li)
    lb_im = mag * jnp.sin(li)
    nr = lb_re - 1.0
    ni = lb_im
    den = a_re * a_re + a_im * a_im
    coef_re = (nr * a_re + ni * a_im) / den
    coef_im = (ni * a_re - nr * a_im) / den
    bb_re = coef_re[..., None] * b_re - coef_im[..., None] * b_im
    bb_im = coef_re[..., None] * b_im + coef_im[..., None] * b_re

    n = jnp.arange(S5_CHUNK + 1, dtype=jnp.float32)
    pmag = jnp.exp(lr[..., None] * n)
    pw_re = pmag * jnp.cos(li[..., None] * n)
    pw_im = pmag * jnp.sin(li[..., None] * n)

    ct_re = jnp.tile(jnp.swapaxes(c_re, -1, -2), (1, 1, 1, S5_CHUNK + 1))
    ct_im = jnp.tile(jnp.swapaxes(c_im, -1, -2), (1, 1, 1, S5_CHUNK + 1))
    pr_re = jnp.repeat(pw_re, S5_GROUP, axis=-1)
    pr_im = jnp.repeat(pw_im, S5_GROUP, axis=-1)
    cp_re = ct_re * pr_re - ct_im * pr_im
    cp_im = ct_re * pr_im + ct_im * pr_re
    lead = cp_re.shape[:3]
    cp0 = jnp.stack([cp_re[..., :S5_VEC], cp_im[..., :S5_VEC]], axis=2)
    cp1 = jnp.stack([cp_re[..., S5_GROUP:], cp_im[..., S5_GROUP:]], axis=2)

    bbt_re = jnp.swapaxes(bb_re, -1, -2)
    bbt_im = jnp.swapaxes(bb_im, -1, -2)
    bbt = jnp.stack([bbt_re, bbt_im], axis=2)

    rev_re = jnp.swapaxes(pw_re[..., S5_CHUNK - 1::-1], -1, -2)[..., None, :]
    rev_im = jnp.swapaxes(pw_im[..., S5_CHUNK - 1::-1], -1, -2)[..., None, :]
    wu_re = rev_re * bbt_re[..., None, :, :] - rev_im * bbt_im[..., None, :, :]
    wu_im = rev_re * bbt_im[..., None, :, :] + rev_im * bbt_re[..., None, :, :]
    wu = jnp.concatenate([wu_re, wu_im], axis=-1).reshape(lead[:2] + (S5_VEC, S5_REAL))

    dm = jnp.eye(S5_GROUP, dtype=jnp.float32) * d.reshape(lead[:2] + (1, S5_GROUP))
    dm = jnp.pad(dm, ((0, 0), (0, 0), (0, 0), (0, S5_VEC - S5_GROUP)))

    l_re, l_im = pw_re[..., S5_CHUNK], pw_im[..., S5_CHUNK]
    ar = jnp.concatenate([l_re, l_re], axis=-1)
    ai = jnp.concatenate([-l_im, l_im], axis=-1)
    coef = jnp.broadcast_to(jnp.stack([ar, ai], axis=2)[..., None, :],
                            lead[:2] + (2, BATCH, S5_REAL))
    return {"s5_wu": wu.astype(jnp.bfloat16), "s5_bbt": bbt, "s5_cp0": cp0, "s5_cp1": cp1,
            "s5_dm": dm, "s5_coef": coef}


def kernel(x, p, g_mix, w_in, conv_a_w, conv_a_b, gate_x_w, gate_x_b, gate_a_w, gate_a_b, lru_lambda, w_a_out, s5_a_re, s5_a_im, s5_log_dt, s5_b_re, s5_b_im, s5_c_re, s5_c_im, s5_d, w_glu, b_glu, w_o, g_ffn, w_up, conv_f_w, conv_f_b, w_down, g_ple, w_ple_gate, w_ple_proj, g_final):
    assert x.shape == (BATCH, SEQ, D_MODEL) and p.shape == (DEPTH, BATCH, SEQ, PLE_DIM)
    bf = jnp.bfloat16
    row = lambda v: v[:, None, :]
    ones = functools.partial(jnp.full, dtype=jnp.float32)
    in_scale = jnp.concatenate([ones((O_MA,), 1.0), ones((2 * D_MODEL,), 0.5)])
    glu_scale = jnp.concatenate([ones((D_MODEL,), 0.125), ones((D_MODEL,), 0.25)])
    prm = {
        "g_mix": row(g_mix), "g_ffn": row(g_ffn), "g_ple": row(g_ple), "g_final": g_final[None, None],
        "w_in": (w_in * in_scale).astype(bf), "cw": conv_a_w, "cb": row(conv_a_b),
        "gw": (0.5 * jnp.concatenate([gate_x_w, gate_a_w], axis=-1)).astype(bf),
        "gxb": row(0.5 * gate_x_b), "gab": row(0.5 * gate_a_b), "lam": row(lru_lambda),
        "wao": (0.125 * w_a_out).astype(bf),
        "wglu": (w_glu * glu_scale).astype(bf), "bglu": row(b_glu * (2.0 * glu_scale)),
        "wo": w_o.astype(bf),
        "wup": w_up.astype(bf), "cfw": conv_f_w, "cfb": row(conv_f_b),
        "wdn": (0.5 * w_down).astype(bf), "wpg": (0.5 * w_ple_gate).astype(bf),
        "wpp": (0.5 * w_ple_proj).astype(bf),
    }
    prm.update(_s5_params(s5_a_re, s5_a_im, s5_log_dt, s5_b_re, s5_b_im, s5_c_re, s5_c_im, s5_d))

    xt = x
    for i in range(DEPTH):
        if i == 0:
            ug, ap, sb, xt = _mixer_call(xt, i, prm, batch_major_in=True)
        else:
            ug, ap, sb = _mixer_call(xt, i, prm, batch_major_in=False)
        yg = _s5_call(ug, i, prm)
        x1 = _merge_call(xt, yg, ap, sb, i, prm)
        xt = _ffn_call(x1, p, i, prm, final=(i == DEPTH - 1))
    return xt
```

```python
import functools
import math

import jax
import jax.numpy as jnp
from jax import lax
from jax.experimental import pallas as pl
from jax.experimental.pallas import tpu as pltpu

D_MODEL = 1024
BATCH = 8
SEQ = 4096
DEPTH = 2
PLE_DIM = 256
LRU_WIDTH = 1280
LRU_HEADS = 10
LRU_HEAD_DIM = 128
LRU_CONV = 4
LRU_C = 8.0
S5_GROUP = 16
S5_GROUPS = 64
S5_STATE = 64
D_FF = 3 * D_MODEL
FFN_CONV = 3
EPS = 1e-6

LANES = 128
ROWS = SEQ * BATCH
PAIR = 2 * LRU_HEAD_DIM
N_PAIRS = LRU_WIDTH // PAIR
S5_CHUNK = 16
S5_VEC = S5_CHUNK * S5_GROUP
S5_NCHUNK = SEQ // S5_CHUNK
S5_ROWS = S5_NCHUNK * BATCH
S5_REAL = 2 * S5_STATE
CHUNK_ROWS = S5_CHUNK * BATCH
BLOCKS = LANES // S5_GROUP

O_GB = LRU_WIDTH
O_UB = 2 * LRU_WIDTH
O_MA = O_UB + D_MODEL
O_MB = O_MA + D_MODEL

MIXER_ROWS = 512
MERGE_ROWS = 1024
MERGE_SUB = 256
MERGE_COLS = 256
FFN_ROWS = 1024
FFN_COLS = 256
S5_GROUPS_PER_STEP = 4
VMEM_LIMIT_BYTES = 60 * 1024 * 1024

_GELU_C = math.sqrt(2.0 / math.pi)


def _gelu2(x):
    return x * (1.0 + jnp.tanh(x * (_GELU_C + (_GELU_C * 0.044715) * (x * x))))


def _sigmoid2(half_x):
    return jnp.tanh(half_x) + 1.0


def _rmsnorm(x, g):
    ms = jnp.mean(x * x, axis=-1, keepdims=True)
    return x * lax.rsqrt(ms + EPS) * g


def _dot(a, b):
    return jnp.dot(a, b, preferred_element_type=jnp.float32)


def _layer_spec(w, layer):
    zeros = (0,) * (w.ndim - 1)
    return pl.BlockSpec((None,) + w.shape[1:], lambda i: (layer,) + zeros,
                        pipeline_mode=pl.Buffered(1))


def _row_spec(rows, cols):
    return pl.BlockSpec((rows, cols), lambda i: (i, 0))


def _batch_major_spec(rows, cols):
    return pl.BlockSpec((BATCH, rows // BATCH, cols), lambda i: (0, i, 0))


def _to_time_major(src_ref, dst_ref):
    for t in range(src_ref.shape[1]):
        dst_ref[t * BATCH:(t + 1) * BATCH, :] = src_ref[:, t, :]


def _from_time_major(src, dst_ref):
    for t in range(dst_ref.shape[1]):
        dst_ref[:, t, :] = src[t * BATCH:(t + 1) * BATCH, :]


def _group_spec(rows):
    return pl.BlockSpec((S5_GROUPS, rows // S5_CHUNK, S5_VEC), lambda i: (0, i, 0))


def _lane_roll(a, shift):
    if a.dtype == jnp.float32:
        return pltpu.roll(a, shift, axis=1)
    return jnp.concatenate([a[:, LANES - shift:], a[:, :LANES - shift]], axis=1)


def _block_transpose(v):
    blk = lax.broadcasted_iota(jnp.int32, v[0].shape, 1) // S5_GROUP
    for d in (4, 2, 1):
        bit = (blk & d) != 0
        out = list(v)
        for i in range(BLOCKS):
            if i & d == 0:
                lo, hi = v[i], v[i + d]
                out[i] = jnp.where(bit, _lane_roll(hi, S5_GROUP * d), lo)
                out[i + d] = jnp.where(bit, hi, _lane_roll(lo, LANES - S5_GROUP * d))
        v = out
    return v


def _mixer_kernel(x_ref, g_ref, win_ref, cw_ref, cb_ref, gw_ref, gxb_ref, gab_ref, lam_ref,
                  wao_ref, ug_ref, ap_ref, sb_ref, *rest, batch_major_in):
    if batch_major_in:
        xt_ref, ext_s, a_s, u_s, gb_s, ma_s, ub_s, hg_s, hc_s = rest
        _to_time_major(x_ref, xt_ref)
        x_ref = xt_ref
    else:
        ext_s, a_s, u_s, gb_s, ma_s, ub_s, hg_s, hc_s = rest
    rows = x_ref.shape[0]
    nchunk = rows // CHUNK_ROWS
    halo = (LRU_CONV - 1) * BATCH

    @pl.when(pl.program_id(0) == 0)
    def _():
        ext_s[0:halo, :] = jnp.zeros((halo, LRU_WIDTH), jnp.float32)
        hc_s[...] = jnp.zeros_like(hc_s)

    hb = _rmsnorm(x_ref[...], g_ref[...]).astype(jnp.bfloat16)

    lam = lam_ref[...]
    log_coef = (0.5 * LRU_C) * (jnp.minimum(lam, 0.0) - jnp.log1p(jnp.exp(-jnp.abs(lam))))

    for j in range(N_PAIRS):
        c0 = j * PAIR
        xa = _dot(hb, win_ref[:, c0:c0 + PAIR])
        ext_s[halo:halo + rows, c0:c0 + PAIR] = xa
        xc = cb_ref[:, c0:c0 + PAIR] + cw_ref[3:4, c0:c0 + PAIR] * xa
        for k in range(LRU_CONV - 1):
            xc = xc + cw_ref[k:k + 1, c0:c0 + PAIR] * ext_s[k * BATCH:k * BATCH + rows, c0:c0 + PAIR]
        xcb = xc.astype(jnp.bfloat16)
        for hh in range(2):
            lo = hh * LRU_HEAD_DIM
            cs = slice(c0 + lo, c0 + lo + LRU_HEAD_DIM)
            gates = _dot(xcb[:, lo:lo + LRU_HEAD_DIM], gw_ref[2 * j + hh])
            gx2 = _sigmoid2(gates[:, :LRU_HEAD_DIM] + gxb_ref[:, cs])
            log_a = log_coef[:, cs] * jnp.tanh(gates[:, LRU_HEAD_DIM:] + gab_ref[:, cs]) + log_coef[:, cs]
            a = jnp.exp(log_a)
            mult = jnp.sqrt(jnp.maximum((1.0 - a) * (1.0 + a), 1e-12))
            a_s[:, cs] = a
            u_s[:, cs] = mult * (gx2 * xc[:, lo:lo + LRU_HEAD_DIM])
        gb_s[:, c0:c0 + PAIR] = _gelu2(_dot(hb, win_ref[:, O_GB + c0:O_GB + c0 + PAIR]))

    ext_s[0:halo, :] = ext_s[rows:rows + halo, :]

    ma_s[...] = _sigmoid2(_dot(hb, win_ref[:, O_MA:O_MA + D_MODEL]))
    sb_ref[...] = _sigmoid2(_dot(hb, win_ref[:, O_MB:O_MB + D_MODEL])).astype(sb_ref.dtype)

    ub_s[...] = _dot(hb, win_ref[:, O_UB:O_UB + D_MODEL]).reshape(nchunk, S5_CHUNK, BATCH, D_MODEL)
    for tile in range(D_MODEL // LANES):
        for half in range(S5_CHUNK // BLOCKS):
            v = [ub_s[:, half * BLOCKS + s, :, tile * LANES:(tile + 1) * LANES].reshape(nchunk * BATCH, LANES)
                 for s in range(BLOCKS)]
            w = _block_transpose(v)
            for gi in range(BLOCKS):
                ug_ref[tile * BLOCKS + gi, :, half * LANES:(half + 1) * LANES] = w[gi].astype(ug_ref.dtype)

    h = hc_s[...]
    for t in range(rows // BATCH):
        r = t * BATCH
        h = a_s[r:r + BATCH, :] * h + u_s[r:r + BATCH, :]
        hg_s[r:r + BATCH, :] = h * gb_s[r:r + BATCH, :]
    hc_s[...] = h

    ya = _dot(hg_s[...].astype(jnp.bfloat16), wao_ref[...])
    ap_ref[...] = (ma_s[...] * ya).astype(ap_ref.dtype)


def _mixer_call(x, layer, prm, batch_major_in):
    rows = MIXER_ROWS
    halo = (LRU_CONV - 1) * BATCH
    out = jax.ShapeDtypeStruct((ROWS, D_MODEL), jnp.bfloat16)
    grp = jax.ShapeDtypeStruct((S5_GROUPS, S5_ROWS, S5_VEC), jnp.bfloat16)
    weights = [prm[k] for k in ("g_mix", "w_in", "cw", "cb", "gw", "gxb", "gab", "lam", "wao")]
    f32 = jnp.float32
    x_spec = _batch_major_spec(rows, D_MODEL) if batch_major_in else _row_spec(rows, D_MODEL)
    extra = batch_major_in * [jax.ShapeDtypeStruct((ROWS, D_MODEL), f32)]
    return pl.pallas_call(
        functools.partial(_mixer_kernel, batch_major_in=batch_major_in),
        grid=(ROWS // rows,),
        in_specs=[x_spec] + [_layer_spec(w, layer) for w in weights],
        out_specs=[_group_spec(rows)] + [_row_spec(rows, D_MODEL)] * (2 + len(extra)),
        out_shape=[grp, out, out] + extra,
        scratch_shapes=[
            pltpu.VMEM((halo + rows, LRU_WIDTH), f32),
            pltpu.VMEM((rows, LRU_WIDTH), f32),
            pltpu.VMEM((rows, LRU_WIDTH), f32),
            pltpu.VMEM((rows, LRU_WIDTH), f32),
            pltpu.VMEM((rows, D_MODEL), f32),
            pltpu.VMEM((rows // CHUNK_ROWS, S5_CHUNK, BATCH, D_MODEL), f32),
            pltpu.VMEM((rows, LRU_WIDTH), f32),
            pltpu.VMEM((BATCH, LRU_WIDTH), f32),
        ],
        compiler_params=pltpu.CompilerParams(
            dimension_semantics=("arbitrary",), vmem_limit_bytes=VMEM_LIMIT_BYTES),
        name="mixer_in" if batch_major_in else "mixer",
    )(x, *weights)


def _s5_kernel(u_ref, wu_ref, bbt_ref, cp0_ref, cp1_ref, dm_ref, coef_ref, y_ref,
               v_s, vsw_s, xs_s, tw_s):
    ngroups = u_ref.shape[0]
    hi = lax.Precision.HIGHEST
    lane = lax.broadcasted_iota(jnp.int32, (S5_GROUP, S5_VEC), 1)
    for g in range(ngroups):
        taps = (jnp.dot(bbt_ref[g, 0], cp0_ref[g, 0], precision=hi, preferred_element_type=jnp.float32)
                - jnp.dot(bbt_ref[g, 1], cp0_ref[g, 1], precision=hi, preferred_element_type=jnp.float32)
                + dm_ref[g])
        tw_s[g, 0:S5_GROUP, :] = taps.astype(jnp.bfloat16)
        for s in range(1, S5_CHUNK):
            shifted = jnp.where(lane >= s * S5_GROUP, pltpu.roll(taps, s * S5_GROUP, axis=1), 0.0)
            tw_s[g, s * S5_GROUP:(s + 1) * S5_GROUP, :] = shifted.astype(jnp.bfloat16)
        tw_s[g, S5_VEC:S5_VEC + S5_STATE, :] = cp1_ref[g, 0].astype(jnp.bfloat16)
        tw_s[g, S5_VEC + S5_STATE:, :] = (-cp1_ref[g, 1]).astype(jnp.bfloat16)

        v = _dot(u_ref[g], wu_ref[g])
        v_s[g] = v
        vsw_s[g] = pltpu.roll(v, S5_STATE, axis=1)

    zero = jnp.zeros((BATCH, S5_REAL), jnp.float32)
    state = [(zero, zero)] * ngroups
    for k in range(S5_NCHUNK):
        rs = slice(k * BATCH, (k + 1) * BATCH)
        for g in range(ngroups):
            x, xsw = state[g]
            xs_s[g, rs, :] = x
            ar = coef_ref[g, 0]
            ai = coef_ref[g, 1]
            state[g] = (ar * x + ai * xsw + v_s[g, rs, :], ar * xsw - ai * x + vsw_s[g, rs, :])

    for g in range(ngroups):
        y = _dot(u_ref[g], tw_s[g, 0:S5_VEC, :])
        y = y + _dot(xs_s[g].astype(jnp.bfloat16), tw_s[g, S5_VEC:, :])
        y_ref[g] = y.astype(y_ref.dtype)


def _s5_call(u_grp, layer, prm):
    gp = S5_GROUPS_PER_STEP
    weights = [prm[k] for k in ("s5_wu", "s5_bbt", "s5_cp0", "s5_cp1", "s5_dm", "s5_coef")]

    def wspec(w):
        zeros = (0,) * (w.ndim - 2)
        return pl.BlockSpec((None, gp) + w.shape[2:], lambda i: (layer, i) + zeros)

    blk = pl.BlockSpec((gp, S5_ROWS, S5_VEC), lambda i: (i, 0, 0))
    return pl.pallas_call(
        _s5_kernel,
        grid=(S5_GROUPS // gp,),
        in_specs=[blk] + [wspec(w) for w in weights],
        out_specs=blk,
        out_shape=jax.ShapeDtypeStruct((S5_GROUPS, S5_ROWS, S5_VEC), jnp.bfloat16),
        scratch_shapes=[pltpu.VMEM((gp, S5_ROWS, S5_REAL), jnp.float32)] * 3
        + [pltpu.VMEM((gp, S5_VEC + S5_REAL, S5_VEC), jnp.bfloat16)],
        compiler_params=pltpu.CompilerParams(
            dimension_semantics=("arbitrary",), vmem_limit_bytes=VMEM_LIMIT_BYTES),
        name="s5",
    )(u_grp, *weights)


def _merge_kernel(x_ref, yg_ref, ap_ref, sb_ref, wglu_ref, bglu_ref, wo_ref, o_ref, ys_s):
    nchunk = MERGE_SUB // CHUNK_ROWS
    assert nchunk * BATCH == 16
    grows = MERGE_SUB // S5_CHUNK

    def step_major(a):
        a = a.reshape(nchunk, S5_CHUNK, BATCH, D_MODEL)
        return jnp.swapaxes(a, 0, 1).reshape(MERGE_SUB, D_MODEL)

    def chunk_major(a):
        a = a.reshape(S5_CHUNK, nchunk, BATCH, D_MODEL)
        return jnp.swapaxes(a, 0, 1).reshape(MERGE_SUB, D_MODEL)

    for sub in range(x_ref.shape[0] // MERGE_SUB):
        rs = slice(sub * MERGE_SUB, (sub + 1) * MERGE_SUB)
        gs = slice(sub * grows, (sub + 1) * grows)
        for tile in range(D_MODEL // LANES):
            ls = slice(tile * LANES, (tile + 1) * LANES)
            for half in range(S5_CHUNK // BLOCKS):
                w = [yg_ref[tile * BLOCKS + gi, gs, half * LANES:(half + 1) * LANES] for gi in range(BLOCKS)]
                v = _block_transpose(w)
                for s in range(BLOCKS):
                    step = half * BLOCKS + s
                    ys_s[sub, step * nchunk * BATCH:(step + 1) * nchunk * BATCH, ls] = (
                        _gelu2(v[s].astype(jnp.float32)).astype(jnp.bfloat16))

        ys = ys_s[sub]
        ap = step_major(ap_ref[rs, :].astype(jnp.float32))
        sb = step_major(sb_ref[rs, :].astype(jnp.float32))
        blocks = []
        for c0 in range(0, D_MODEL, MERGE_COLS):
            ca = slice(c0, c0 + MERGE_COLS)
            cb = slice(D_MODEL + c0, D_MODEL + c0 + MERGE_COLS)
            glu_a = _dot(ys, wglu_ref[:, ca]) + bglu_ref[:, ca]
            glu_b = _dot(ys, wglu_ref[:, cb]) + bglu_ref[:, cb]
            blocks.append((ap[:, ca] + sb[:, ca] * (glu_a * _sigmoid2(glu_b))).astype(jnp.bfloat16))
        merged = jnp.concatenate(blocks, axis=1)
        o_ref[rs, :] = x_ref[rs, :] + chunk_major(_dot(merged, wo_ref[...]))


def _merge_call(xt, yg, ap, sb, layer, prm):
    rows = MERGE_ROWS
    weights = [prm[k] for k in ("wglu", "bglu", "wo")]
    return pl.pallas_call(
        _merge_kernel,
        grid=(ROWS // rows,),
        in_specs=[_row_spec(rows, D_MODEL), _group_spec(rows), _row_spec(rows, D_MODEL),
                  _row_spec(rows, D_MODEL)] + [_layer_spec(w, layer) for w in weights],
        out_specs=_row_spec(rows, D_MODEL),
        out_shape=jax.ShapeDtypeStruct((ROWS, D_MODEL), jnp.float32),
        scratch_shapes=[pltpu.VMEM((rows // MERGE_SUB, MERGE_SUB, D_MODEL), jnp.bfloat16)],
        compiler_params=pltpu.CompilerParams(
            dimension_semantics=("arbitrary",), vmem_limit_bytes=VMEM_LIMIT_BYTES),
        name="merge",
    )(xt, yg, ap, sb, *weights)


def _ffn_kernel(x_ref, p_ref, gffn_ref, wup_ref, cfw_ref, cfb_ref, wdn_ref,
                gple_ref, wpg_ref, wpp_ref, gfin_ref,
                o_ref, act_s, fc_s, pt_s, *, final):
    rows = x_ref.shape[0]
    halo = (FFN_CONV - 1) * BATCH

    @pl.when(pl.program_id(0) == 0)
    def _():
        fc_s[...] = jnp.zeros_like(fc_s)

    h2 = _rmsnorm(x_ref[...], gffn_ref[...]).astype(jnp.bfloat16)

    def conv_block(col):
        up = _dot(h2, wup_ref[:, col:col + FFN_COLS])
        ext = jnp.concatenate([fc_s[:, col:col + FFN_COLS], up], axis=0)
        fc_s[:, col:col + FFN_COLS] = up[rows - halo:rows, :]
        out = cfb_ref[:, col:col + FFN_COLS] + cfw_ref[2:3, col:col + FFN_COLS] * up
        for k in range(FFN_CONV - 1):
            out = out + cfw_ref[k:k + 1, col:col + FFN_COLS] * ext[k * BATCH:k * BATCH + rows, :]
        return out

    for c in range(D_FF // FFN_COLS):
        col = c * FFN_COLS
        act = _gelu2(conv_block(col)) * conv_block(D_FF + col)
        act_s[:, col:col + FFN_COLS] = act.astype(jnp.bfloat16)

    x2 = x_ref[...] + _dot(act_s[...], wdn_ref[...])

    h3 = _rmsnorm(x2, gple_ref[...]).astype(jnp.bfloat16)
    gate = _sigmoid2(_dot(h3, wpg_ref[...]))
    _to_time_major(p_ref, pt_s)
    x3 = x2 + gate * _dot(pt_s[...].astype(jnp.bfloat16), wpp_ref[...])
    if final:
        x3 = _rmsnorm(x3, gfin_ref[...])
        _from_time_major(x3, o_ref)
    else:
        o_ref[...] = x3


def _ffn_call(x1, p, layer, prm, final):
    rows = FFN_ROWS
    halo = (FFN_CONV - 1) * BATCH
    weights = [prm[k] for k in ("g_ffn", "wup", "cfw", "cfb", "wdn", "g_ple", "wpg", "wpp")]
    return pl.pallas_call(
        functools.partial(_ffn_kernel, final=final),
        grid=(ROWS // rows,),
        in_specs=[_row_spec(rows, D_MODEL),
                  pl.BlockSpec((None, BATCH, rows // BATCH, PLE_DIM), lambda i: (layer, 0, i, 0))]
        + [_layer_spec(w, layer) for w in weights] + [_layer_spec(prm["g_final"], 0)],
        out_specs=_batch_major_spec(rows, D_MODEL) if final else _row_spec(rows, D_MODEL),
        out_shape=jax.ShapeDtypeStruct((BATCH, SEQ, D_MODEL) if final else (ROWS, D_MODEL),
                                       jnp.float32),
        scratch_shapes=[
            pltpu.VMEM((rows, D_FF), jnp.bfloat16),
            pltpu.VMEM((halo, 2 * D_FF), jnp.float32),
            pltpu.VMEM((rows, PLE_DIM), jnp.float32),
        ],
        compiler_params=pltpu.CompilerParams(
            dimension_semantics=("arbitrary",), vmem_limit_bytes=VMEM_LIMIT_BYTES),
        name="ffn_final" if final else "ffn",
    )(x1, p, *weights, prm["g_final"])


def _s5_params(a_re, a_im, log_dt, b_re, b_im, c_re, c_im, d):
    dt = jnp.exp(log_dt)[..., None]
    lr = a_re * dt
    li = a_im * dt
    mag = jnp.exp(lr)
    lb_re = mag * jnp.cos(li)
    lb_im = mag * jnp.sin(li)
    nr = lb_re - 1.0
    ni = lb_im
    den = a_re * a_re + a_im * a_im
    coef_re = (nr * a_re + ni * a_im) / den
    coef_im = (ni * a_re - nr * a_im) / den
    bb_re = coef_re[..., None] * b_re - coef_im[..., None] * b_im
    bb_im = coef_re[..., None] * b_im + coef_im[..., None] * b_re

    n = jnp.arange(S5_CHUNK + 1, dtype=jnp.float32)
    pmag = jnp.exp(lr[..., None] * n)
    pw_re = pmag * jnp.cos(li[..., None] * n)
    pw_im = pmag * jnp.sin(li[..., None] * n)

    ct_re = jnp.tile(jnp.swapaxes(c_re, -1, -2), (1, 1, 1, S5_CHUNK + 1))
    ct_im = jnp.tile(jnp.swapaxes(c_im, -1, -2), (1, 1, 1, S5_CHUNK + 1))
    pr_re = jnp.repeat(pw_re, S5_GROUP, axis=-1)
    pr_im = jnp.repeat(pw_im, S5_GROUP, axis=-1)
    cp_re = ct_re * pr_re - ct_im * pr_im
    cp_im = ct_re * pr_im + ct_im * pr_re
    lead = cp_re.shape[:3]
    cp0 = jnp.stack([cp_re[..., :S5_VEC], cp_im[..., :S5_VEC]], axis=2)
    cp1 = jnp.stack([cp_re[..., S5_GROUP:], cp_im[..., S5_GROUP:]], axis=2)

    bbt_re = jnp.swapaxes(bb_re, -1, -2)
    bbt_im = jnp.swapaxes(bb_im, -1, -2)
    bbt = jnp.stack([bbt_re, bbt_im], axis=2)

    rev_re = jnp.swapaxes(pw_re[..., S5_CHUNK - 1::-1], -1, -2)[..., None, :]
    rev_im = jnp.swapaxes(pw_im[..., S5_CHUNK - 1::-1], -1, -2)[..., None, :]
    wu_re = rev_re * bbt_re[..., None, :, :] - rev_im * bbt_im[..., None, :, :]
    wu_im = rev_re * bbt_im[..., None, :, :] + rev_im * bbt_re[..., None, :, :]
    wu = jnp.concatenate([wu_re, wu_im], axis=-1).reshape(lead[:2] + (S5_VEC, S5_REAL))

    dm = jnp.eye(S5_GROUP, dtype=jnp.float32) * d.reshape(lead[:2] + (1, S5_GROUP))
    dm = jnp.pad(dm, ((0, 0), (0, 0), (0, 0), (0, S5_VEC - S5_GROUP)))

    l_re, l_im = pw_re[..., S5_CHUNK], pw_im[..., S5_CHUNK]
    ar = jnp.concatenate([l_re, l_re], axis=-1)
    ai = jnp.concatenate([-l_im, l_im], axis=-1)
    coef = jnp.broadcast_to(jnp.stack([ar, ai], axis=2)[..., None, :],
                            lead[:2] + (2, BATCH, S5_REAL))
    return {"s5_wu": wu.astype(jnp.bfloat16), "s5_bbt": bbt, "s5_cp0": cp0, "s5_cp1": cp1,
            "s5_dm": dm, "s5_coef": coef}


def kernel(x, p, g_mix, w_in, conv_a_w, conv_a_b, gate_x_w, gate_x_b, gate_a_w, gate_a_b, lru_lambda, w_a_out, s5_a_re, s5_a_im, s5_log_dt, s5_b_re, s5_b_im, s5_c_re, s5_c_im, s5_d, w_glu, b_glu, w_o, g_ffn, w_up, conv_f_w, conv_f_b, w_down, g_ple, w_ple_gate, w_ple_proj, g_final):
    assert x.shape == (BATCH, SEQ, D_MODEL) and p.shape == (DEPTH, BATCH, SEQ, PLE_DIM)
    bf = jnp.bfloat16
    row = lambda v: v[:, None, :]
    ones = functools.partial(jnp.full, dtype=jnp.float32)
    in_scale = jnp.concatenate([ones((O_MA,), 1.0), ones((2 * D_MODEL,), 0.5)])
    glu_scale = jnp.concatenate([ones((D_MODEL,), 0.125), ones((D_MODEL,), 0.25)])
    prm = {
        "g_mix": row(g_mix), "g_ffn": row(g_ffn), "g_ple": row(g_ple), "g_final": g_final[None, None],
        "w_in": (w_in * in_scale).astype(bf), "cw": conv_a_w, "cb": row(conv_a_b),
        "gw": (0.5 * jnp.concatenate([gate_x_w, gate_a_w], axis=-1)).astype(bf),
        "gxb": row(0.5 * gate_x_b), "gab": row(0.5 * gate_a_b), "lam": row(lru_lambda),
        "wao": (0.125 * w_a_out).astype(bf),
        "wglu": (w_glu * glu_scale).astype(bf), "bglu": row(b_glu * (2.0 * glu_scale)),
        "wo": w_o.astype(bf),
        "wup": w_up.astype(bf), "cfw": conv_f_w, "cfb": row(conv_f_b),
        "wdn": (0.5 * w_down).astype(bf), "wpg": (0.5 * w_ple_gate).astype(bf),
        "wpp": (0.5 * w_ple_proj).astype(bf),
    }
    prm.update(_s5_params(s5_a_re, s5_a_im, s5_log_dt, s5_b_re, s5_b_im, s5_c_re, s5_c_im, s5_d))

    xt = x
    for i in range(DEPTH):
        if i == 0:
            ug, ap, sb, xt = _mixer_call(xt, i, prm, batch_major_in=True)
        else:
            ug, ap, sb = _mixer_call(xt, i, prm, batch_major_in=False)
        yg = _s5_call(ug, i, prm)
        x1 = _merge_call(xt, yg, ap, sb, i, prm)
        xt = _ffn_call(x1, p, i, prm, final=(i == DEPTH - 1))
    return xt
```
